```python
import math, functools
import jax, jax.numpy as jnp
from jax import lax
import numpy as np

D_MODEL = 2048
BATCH = 2
SEQ = 4096
DEPTH = 2
DEC_BATCH = 8
DEC_SEQ = 4
PAST_LEN = 16384
PAGE_SIZE = 128

MIX_WIDTH = D_MODEL
HEAD_DIM = 128
ATT_WIDTH = D_MODEL // 2
ATT_HEADS = ATT_WIDTH // HEAD_DIM
IDX_HEADS = 16
IDX_DIM = 64
TOPK_MAX = 256
Q_BLOCK = 128
ROPE_THETA = 10000.0
M_WIDTH = D_MODEL // 4
M_HEAD_DIM = 64
M_HEADS = M_WIDTH // M_HEAD_DIM
M_GROUPS = 2
M_STATE = 128
M_CONV = 4
M_CONV_CH = M_WIDTH + 2 * M_GROUPS * M_STATE
SSD_CHUNK = 128
S5_WIDTH = MIX_WIDTH - ATT_WIDTH - M_WIDTH
S5_GROUP = 16
S5_GROUPS = S5_WIDTH // S5_GROUP
S5_STATE = 64
D_FF = (D_MODEL * 11) // 4
FFN_CONV = 3
EPS = 1e-6
IN_SIZES = (ATT_WIDTH, ATT_WIDTH, ATT_WIDTH, IDX_HEADS * IDX_DIM, IDX_DIM, IDX_HEADS,
            M_WIDTH, M_CONV_CH, M_HEADS, S5_WIDTH)
N_IN = sum(IN_SIZES)

kernel_name = "hybrid_dsa_ssd_s5_step"


def rms_norm(x, g):
    xf = x.astype(jnp.float32)
    y = xf * lax.rsqrt(jnp.mean(xf * xf, axis=-1, keepdims=True) + EPS)
    return (y * g.astype(jnp.float32)).astype(x.dtype)


def rope(x, pos):
    half = x.shape[-1] // 2
    inv = ROPE_THETA ** (-jnp.arange(half, dtype=jnp.float32) / half)
    ang = pos.astype(jnp.float32)[:, None] * inv[None, :]
    cos, sin = jnp.cos(ang)[:, None, :], jnp.sin(ang)[:, None, :]
    xf = x.astype(jnp.float32)
    x1, x2 = xf[..., :half], xf[..., half:]
    return jnp.concatenate([x1 * cos - x2 * sin, x2 * cos + x1 * sin], axis=-1).astype(x.dtype)


def split_cols(x, sizes):
    out, start = [], 0
    for s in sizes:
        out.append(x[..., start:start + s])
        start += s
    return out


def causal_dwconv(x, prev, w, b):
    width, T = w.shape[0], x.shape[1]
    xx = jnp.concatenate([prev.astype(x.dtype), x], axis=1)
    y = b.astype(x.dtype)
    for j in range(width):
        y = y + xx[:, j:j + T] * w[j].astype(x.dtype)
    return y, xx[:, T:]


def index_scores(iq, iw, ik):
    dots = jax.nn.relu(jnp.einsum('bthd,bsd->bths', iq, ik).astype(jnp.float32) * (IDX_DIM ** -0.5))
    return jnp.einsum('bth,bths->bts', iw.astype(jnp.float32) * (IDX_HEADS ** -0.5), dots)


def select_keys(scores, qpos, topk):
    L = scores.shape[-1]
    visible = jnp.arange(L, dtype=jnp.int32)[None, None, :] <= qpos[None, :, None]
    _, idx = lax.top_k(jnp.where(visible, scores, -jnp.inf), topk)
    return idx, idx <= qpos[None, :, None]


def sparse_attend(q, k_sel, v_sel, valid):
    logits = jnp.einsum('bthd,btkhd->bthk', q, k_sel).astype(jnp.float32) * (HEAD_DIM ** -0.5)
    logits = jnp.where(valid[:, :, None, :], logits, -jnp.inf)
    prob = jax.nn.softmax(logits, axis=-1).astype(v_sel.dtype)
    return jnp.einsum('bthk,btkhd->bthd', prob, v_sel)


gather_rows = jax.vmap(lambda rows, ii: rows[ii])


def attn_prompt(q, k, v, iq, ik, iw):
    b, S = q.shape[:2]
    topk = min(TOPK_MAX, S // 4)
    nb = S // Q_BLOCK

    def blocks(a):
        return a.reshape((b, nb, Q_BLOCK) + a.shape[2:]).swapaxes(0, 1)

    def one_block(args):
        qi, iqi, iwi, start = args
        qpos = start + jnp.arange(Q_BLOCK, dtype=jnp.int32)
        idx, valid = select_keys(index_scores(iqi, iwi, ik), qpos, topk)
        return sparse_attend(qi, gather_rows(k, idx), gather_rows(v, idx), valid)

    starts = jnp.arange(nb, dtype=jnp.int32) * Q_BLOCK
    out = lax.map(one_block, (blocks(q), blocks(iq), blocks(iw), starts))
    return out.swapaxes(0, 1).reshape(b, S, ATT_WIDTH)


def attn_sample(q, k, v, iq, ik, iw, *, pool_k, pool_v, pool_ik, page_table, layer):
    b, T = q.shape[:2]
    past = page_table.shape[1] * PAGE_SIZE
    topk = min(TOPK_MAX, (past + T) // 4)
    ik_past = pool_ik[layer, page_table].reshape(b, past, IDX_DIM).astype(ik.dtype)
    ik_all = jnp.concatenate([ik_past, ik], axis=1)
    qpos = past + jnp.arange(T, dtype=jnp.int32)
    idx, valid = select_keys(index_scores(iq, iw, ik_all), qpos, topk)
    is_new = idx >= past
    past_idx = jnp.minimum(idx, past - 1)
    phys = jax.vmap(lambda pt, pg: pt[pg])(page_table, past_idx // PAGE_SIZE)
    off = past_idx % PAGE_SIZE
    new_idx = jnp.clip(idx - past, 0, T - 1)

    def pick(pool, new):
        return jnp.where(is_new[..., None, None], gather_rows(new, new_idx),
                         pool[layer, phys, off].astype(new.dtype))

    out = sparse_attend(q, pick(pool_k, k), pick(pool_v, v), valid)
    return out.reshape(b, T, ATT_WIDTH)


def ssd_scan(x, dt, a_head, b_in, c_in, h0, chunk):
    nb, T, H, P = x.shape
    G, N = b_in.shape[2], b_in.shape[3]
    nc = T // chunk
    bh = jnp.repeat(b_in, H // G, axis=2).reshape(nb, nc, chunk, H, N)
    ch = jnp.repeat(c_in, H // G, axis=2).reshape(nb, nc, chunk, H, N)
    dtc = dt.reshape(nb, nc, chunk, H)
    xdt = x.reshape(nb, nc, chunk, H, P) * dtc[..., None]
    acum = jnp.cumsum(dtc * a_head, axis=2)
    causal = jnp.tril(jnp.ones((chunk, chunk), dtype=bool))[None, None, :, :, None]
    seg = acum[:, :, :, None, :] - acum[:, :, None, :, :]
    scores = jnp.einsum('bclhn,bcshn->bclsh', ch, bh) * jnp.exp(jnp.where(causal, seg, -jnp.inf))
    y_diag = jnp.einsum('bclsh,bcshp->bclhp', scores, xdt)
    decay_to_end = jnp.exp(acum[:, :, -1:, :] - acum)
    chunk_states = jnp.einsum('bclhn,bclhp->bchpn', bh * decay_to_end[..., None], xdt)
    chunk_decay = jnp.exp(acum[:, :, -1, :])

    def step(h, inp):
        s, d = inp
        return h * d[:, :, None, None] + s, h

    h_last, h_in = lax.scan(step, h0, (chunk_states.swapaxes(0, 1), chunk_decay.swapaxes(0, 1)))
    y_off = jnp.einsum('bclhn,bchpn->bclhp', ch * jnp.exp(acum)[..., None], h_in.swapaxes(0, 1))
    return (y_diag + y_off).reshape(nb, T, H, P), h_last


def mamba_mix(z, xbc, dt_raw, conv_prev, h0, chunk, p):
    b, T = z.shape[:2]
    xbc_c, conv_new = causal_dwconv(xbc, conv_prev, p['m_conv_w'], p['m_conv_b'])
    xbc_c = jax.nn.silu(xbc_c.astype(jnp.float32))
    xs, bm, cm = split_cols(xbc_c, (M_WIDTH, M_GROUPS * M_STATE, M_GROUPS * M_STATE))
    xh = xs.reshape(b, T, M_HEADS, M_HEAD_DIM)
    dt = jax.nn.softplus(dt_raw.astype(jnp.float32) + p['m_dt_bias'].astype(jnp.float32))
    a_head = -jnp.exp(p['m_a_log'].astype(jnp.float32))
    y, h_last = ssd_scan(xh, dt, a_head, bm.reshape(b, T, M_GROUPS, M_STATE),
                         cm.reshape(b, T, M_GROUPS, M_STATE), h0.astype(jnp.float32), chunk)
    y = y + p['m_d'].astype(jnp.float32)[:, None] * xh
    y = y.reshape(b, T, M_WIDTH) * jax.nn.silu(z.astype(jnp.float32))
    y = rms_norm(y.reshape(b, T, M_GROUPS, M_WIDTH // M_GROUPS),
                 p['m_norm_g'].reshape(M_GROUPS, M_WIDTH // M_GROUPS)).reshape(b, T, M_WIDTH)
    return y.astype(z.dtype), conv_new, h_last


def complex_affine_combine(e1, e2):
    a1r, a1i, b1r, b1i = e1
    a2r, a2i, b2r, b2i = e2
    return (a2r * a1r - a2i * a1i, a2r * a1i + a2i * a1r,
            a2r * b1r - a2i * b1i + b2r, a2r * b1i + a2i * b1r + b2i)


def s5_mix(u, h0_re, h0_im, p):
    f32 = jnp.float32
    b, T = u.shape[:2]
    uf = u.astype(f32)
    ug = uf.reshape(b, T, S5_GROUPS, S5_GROUP)
    lr, li = p['s5_lam_re'].astype(f32), p['s5_lam_im'].astype(f32)
    step = jnp.exp(p['s5_log_step'].astype(f32))[:, None]
    mag = jnp.exp(lr * step)
    ar, ai = mag * jnp.cos(li * step), mag * jnp.sin(li * step)
    den = lr * lr + li * li
    gr = ((ar - 1.0) * lr + ai * li) / den
    gi = (ai * lr - (ar - 1.0) * li) / den
    bu_re = jnp.einsum('gnk,btgk->btgn', p['s5_b_re'].astype(f32), ug)
    bu_im = jnp.einsum('gnk,btgk->btgn', p['s5_b_im'].astype(f32), ug)
    br = gr * bu_re - gi * bu_im
    bi = gr * bu_im + gi * bu_re
    h0r, h0i = h0_re.astype(f32), h0_im.astype(f32)
    br = br.at[:, 0].add(ar * h0r - ai * h0i)
    bi = bi.at[:, 0].add(ar * h0i + ai * h0r)
    _, _, hr, hi = lax.associative_scan(
        complex_affine_combine,
        (jnp.broadcast_to(ar, br.shape), jnp.broadcast_to(ai, bi.shape), br, bi), axis=1)
    y = (jnp.einsum('gkn,btgn->btgk', p['s5_c_re'].astype(f32), hr)
         - jnp.einsum('gkn,btgn->btgk', p['s5_c_im'].astype(f32), hi)).reshape(b, T, S5_WIDTH)
    y = y + p['s5_d'].astype(f32) * uf
    g = jax.nn.gelu(y)
    out = g * jax.nn.sigmoid(g @ p['s5_glu_w'].astype(f32) + p['s5_glu_b'].astype(f32))
    return out.astype(u.dtype), hr[:, -1], hi[:, -1]


def trunk_layer(x, pos, attend, mconv_prev, ssm_h0, s5_h0_re, s5_h0_im, fconv_prev, p):
    b, T = x.shape[:2]
    h = rms_norm(x, p['norm1_g'])
    q, k, v, iq, ik, iw, z, xbc, dt_raw, u = split_cols(h @ p['w_in'], IN_SIZES)
    q = rope(q.reshape(b, T, ATT_HEADS, HEAD_DIM), pos)
    k = rope(k.reshape(b, T, ATT_HEADS, HEAD_DIM), pos)
    v = v.reshape(b, T, ATT_HEADS, HEAD_DIM)
    iq = rope(iq.reshape(b, T, IDX_HEADS, IDX_DIM), pos)
    ik = rope(ik[:, :, None, :], pos)[:, :, 0]
    att = attend(q, k, v, iq, ik, iw)
    mam, mconv_new, ssm_new = mamba_mix(z, xbc, dt_raw, mconv_prev, ssm_h0, math.gcd(SSD_CHUNK, T), p)
    s5o, s5_re, s5_im = s5_mix(u, s5_h0_re, s5_h0_im, p)
    x = x + jnp.concatenate([att, mam, s5o], axis=-1) @ p['w_out']
    h2 = rms_norm(x, p['norm2_g'])
    up, fconv_new = causal_dwconv(h2 @ p['ffn_w_up'], fconv_prev, p['ffn_conv_w'], p['ffn_conv_b'])
    gate, val = jnp.split(up, 2, axis=-1)
    x = x + (jax.nn.silu(gate) * val) @ p['ffn_w_down']
    return x, (k, v, ik, ssm_new, mconv_new, s5_re, s5_im, fconv_new)


def setup_inputs(seed: int = 0) -> dict:
    key = jax.random.key(seed)
    ks = iter(jax.random.split(key, 48))
    f32 = jnp.float32

    def nrm(shape, scale):
        return jax.random.normal(next(ks), shape, f32) * scale

    def unif(shape, lo, hi):
        return jax.random.uniform(next(ks), shape, f32, lo, hi)

    n_pages = PAST_LEN // PAGE_SIZE
    n_used = DEC_BATCH * n_pages
    n_pool = n_used + max(1, n_used // 4)
    page_table = jax.random.permutation(next(ks), n_pool)[:n_used].reshape(DEC_BATCH, n_pages).astype(jnp.int32)

    x_prompt = nrm((BATCH, SEQ, D_MODEL), 1.0)
    x_sample = nrm((DEC_BATCH, DEC_SEQ, D_MODEL), 1.0)
    cache_k = nrm((DEPTH, n_pool, PAGE_SIZE, ATT_HEADS, HEAD_DIM), 1.0)
    cache_v = nrm((DEPTH, n_pool, PAGE_SIZE, ATT_HEADS, HEAD_DIM), 1.0)
    cache_idx_k = nrm((DEPTH, n_pool, PAGE_SIZE, IDX_DIM), 1.0)
    state_ssm = nrm((DEPTH, DEC_BATCH, M_HEADS, M_HEAD_DIM, M_STATE), 0.1)
    state_mconv = nrm((DEPTH, DEC_BATCH, M_CONV - 1, M_CONV_CH), 1.0)
    state_s5_re = nrm((DEPTH, DEC_BATCH, S5_GROUPS, S5_STATE), 0.1)
    state_s5_im = nrm((DEPTH, DEC_BATCH, S5_GROUPS, S5_STATE), 0.1)
    state_fconv = nrm((DEPTH, DEC_BATCH, FFN_CONV - 1, 2 * D_FF), 1.0)

    dt0 = jnp.exp(unif((DEPTH, M_HEADS), math.log(1e-3), math.log(1e-1)))
    return {
        "x_prompt": x_prompt, "x_sample": x_sample,
        "cache_k": cache_k, "cache_v": cache_v, "cache_idx_k": cache_idx_k,
        "state_ssm": state_ssm, "state_mconv": state_mconv,
        "state_s5_re": state_s5_re, "state_s5_im": state_s5_im, "state_fconv": state_fconv,
        "page_table": page_table,
        "norm1_g": 1.0 + nrm((DEPTH, D_MODEL), 0.02),
        "w_in": nrm((DEPTH, D_MODEL, N_IN), D_MODEL ** -0.5),
        "m_conv_w": nrm((DEPTH, M_CONV, M_CONV_CH), M_CONV ** -0.5),
        "m_conv_b": nrm((DEPTH, M_CONV_CH), 0.02),
        "m_dt_bias": dt0 + jnp.log(-jnp.expm1(-dt0)),
        "m_a_log": jnp.log(unif((DEPTH, M_HEADS), 1.0, 16.0)),
        "m_d": 1.0 + nrm((DEPTH, M_HEADS), 0.02),
        "m_norm_g": 1.0 + nrm((DEPTH, M_WIDTH), 0.02),
        "s5_lam_re": -0.5 + nrm((DEPTH, S5_GROUPS, S5_STATE), 0.01),
        "s5_lam_im": math.pi * jnp.arange(S5_STATE, dtype=f32) + nrm((DEPTH, S5_GROUPS, S5_STATE), 0.01),
        "s5_log_step": unif((DEPTH, S5_GROUPS), math.log(1e-3), math.log(1e-1)),
        "s5_b_re": nrm((DEPTH, S5_GROUPS, S5_STATE, S5_GROUP), (2 * S5_GROUP) ** -0.5),
        "s5_b_im": nrm((DEPTH, S5_GROUPS, S5_STATE, S5_GROUP), (2 * S5_GROUP) ** -0.5),
        "s5_c_re": nrm((DEPTH, S5_GROUPS, S5_GROUP, S5_STATE), S5_STATE ** -0.5),
        "s5_c_im": nrm((DEPTH, S5_GROUPS, S5_GROUP, S5_STATE), S5_STATE ** -0.5),
        "s5_d": nrm((DEPTH, S5_WIDTH), 1.0),
        "s5_glu_w": nrm((DEPTH, S5_WIDTH, S5_WIDTH), S5_WIDTH ** -0.5),
        "s5_glu_b": nrm((DEPTH, S5_WIDTH), 0.02),
        "w_out": nrm((DEPTH, MIX_WIDTH, D_MODEL), MIX_WIDTH ** -0.5),
        "norm2_g": 1.0 + nrm((DEPTH, D_MODEL), 0.02),
        "ffn_w_up": nrm((DEPTH, D_MODEL, 2 * D_FF), D_MODEL ** -0.5),
        "ffn_conv_w": nrm((DEPTH, FFN_CONV, 2 * D_FF), FFN_CONV ** -0.5),
        "ffn_conv_b": nrm((DEPTH, 2 * D_FF), 0.02),
        "ffn_w_down": nrm((DEPTH, D_FF, D_MODEL), D_FF ** -0.5),
        "final_norm_g": 1.0 + nrm((D_MODEL,), 0.02),
    }


def reference(x_prompt, x_sample, cache_k, cache_v, cache_idx_k, state_ssm, state_mconv,
              state_s5_re, state_s5_im, state_fconv, page_table,
              norm1_g, w_in, m_conv_w, m_conv_b, m_dt_bias, m_a_log, m_d, m_norm_g,
              s5_lam_re, s5_lam_im, s5_log_step, s5_b_re, s5_b_im, s5_c_re, s5_c_im,
              s5_d, s5_glu_w, s5_glu_b, w_out, norm2_g, ffn_w_up, ffn_conv_w, ffn_conv_b,
              ffn_w_down, final_norm_g):
    bp, S = x_prompt.shape[:2]
    T = x_sample.shape[1]
    past = page_table.shape[1] * PAGE_SIZE
    pos_p = jnp.arange(S, dtype=jnp.int32)
    pos_s = past + jnp.arange(T, dtype=jnp.int32)
    dt_p = x_prompt.dtype
    zero_mconv = jnp.zeros((bp, M_CONV - 1, M_CONV_CH), dt_p)
    zero_ssm = jnp.zeros((bp, M_HEADS, M_HEAD_DIM, M_STATE), jnp.float32)
    zero_s5 = jnp.zeros((bp, S5_GROUPS, S5_STATE), jnp.float32)
    zero_fconv = jnp.zeros((bp, FFN_CONV - 1, 2 * D_FF), dt_p)

    hp, hs = x_prompt, x_sample
    new_p, new_s = [], []
    for l in range(DEPTH):
        p = {"norm1_g": norm1_g[l], "w_in": w_in[l], "m_conv_w": m_conv_w[l], "m_conv_b": m_conv_b[l],
             "m_dt_bias": m_dt_bias[l], "m_a_log": m_a_log[l], "m_d": m_d[l], "m_norm_g": m_norm_g[l],
             "s5_lam_re": s5_lam_re[l], "s5_lam_im": s5_lam_im[l], "s5_log_step": s5_log_step[l],
             "s5_b_re": s5_b_re[l], "s5_b_im": s5_b_im[l], "s5_c_re": s5_c_re[l], "s5_c_im": s5_c_im[l],
             "s5_d": s5_d[l], "s5_glu_w": s5_glu_w[l], "s5_glu_b": s5_glu_b[l], "w_out": w_out[l],
             "norm2_g": norm2_g[l], "ffn_w_up": ffn_w_up[l], "ffn_conv_w": ffn_conv_w[l],
             "ffn_conv_b": ffn_conv_b[l], "ffn_w_down": ffn_w_down[l]}
        hp, st_p = trunk_layer(hp, pos_p, attn_prompt, zero_mconv, zero_ssm, zero_s5, zero_s5, zero_fconv, p)
        sample_attend = functools.partial(attn_sample, pool_k=cache_k, pool_v=cache_v, pool_ik=cache_idx_k,
                                          page_table=page_table, layer=l)
        hs, st_s = trunk_layer(hs, pos_s, sample_attend, state_mconv[l], state_ssm[l], state_s5_re[l],
                               state_s5_im[l], state_fconv[l], p)
        new_p.append(st_p)
        new_s.append(st_s)

    def stacked(states, i):
        return jnp.stack([s[i] for s in states], axis=0)

    y_prompt = rms_norm(hp, final_norm_g)
    y_sample = rms_norm(hs, final_norm_g)
    return (y_prompt, y_sample,
            stacked(new_p, 0), stacked(new_p, 1), stacked(new_p, 2), stacked(new_p, 3),
            stacked(new_p, 4), stacked(new_p, 5), stacked(new_p, 6), stacked(new_p, 7),
            stacked(new_s, 0), stacked(new_s, 1), stacked(new_s, 2), stacked(new_s, 3),
            stacked(new_s, 4), stacked(new_s, 5), stacked(new_s, 6), stacked(new_s, 7))
```

```python
import functools
import math

import jax
import jax.numpy as jnp
from jax import lax
from jax.experimental import pallas as pl
from jax.experimental.pallas import tpu as pltpu

F32 = jnp.float32
BF16 = jnp.bfloat16
I32 = jnp.int32

D_MODEL = 2048
HEAD_DIM = 128
ATT_WIDTH = D_MODEL // 2
ATT_HEADS = ATT_WIDTH // HEAD_DIM
IDX_HEADS = 16
IDX_DIM = 64
TOPK_MAX = 256
ROPE_THETA = 10000.0
PAGE_SIZE = 128
M_WIDTH = D_MODEL // 4
M_HEAD_DIM = 64
M_HEADS = M_WIDTH // M_HEAD_DIM
M_GROUPS = 2
M_STATE = 128
M_CONV = 4
M_CONV_CH = M_WIDTH + 2 * M_GROUPS * M_STATE
S5_WIDTH = D_MODEL - ATT_WIDTH - M_WIDTH
S5_GROUP = 16
S5_GROUPS = S5_WIDTH // S5_GROUP
S5_STATE = 64
S5_N = S5_GROUPS * S5_STATE
D_FF = (D_MODEL * 11) // 4
FFN_CONV = 3
EPS = 1e-6
IN_SIZES = (ATT_WIDTH, ATT_WIDTH, ATT_WIDTH, IDX_HEADS * IDX_DIM, IDX_DIM, IDX_HEADS,
            M_WIDTH, M_CONV_CH, M_HEADS, S5_WIDTH)

LANES = 128
SUBLANES = 8
VMEM_LIMIT = 48 * 1024 * 1024

C_Q, C_K, C_V, C_IQ, C_XBC, C_Z, C_U, C_IK, C_SM = 0, 1024, 2048, 3072, 4096, 5120, 5632, 6144, 6272
N_P = 6656
PROJ_TN = 512
SM_IW = 0
SM_DT = 16
SSD_L = 128
NEG_BIG = -1e30
INT_MIN = -2 ** 31


def _cparams(sem):
    return pltpu.CompilerParams(dimension_semantics=sem, vmem_limit_bytes=VMEM_LIMIT)


def _iota(shape, dim):
    return lax.broadcasted_iota(I32, shape, dim)


def _sigmoid(x):
    return 1.0 / (1.0 + jnp.exp(-x))


def _nt_dot(a, b):
    return lax.dot_general(a, b, (((1,), (1,)), ((), ())), preferred_element_type=F32)


def _rmsnorm_kernel(x_ref, g_ref, o_ref):
    x = x_ref[...]
    ms = jnp.mean(x * x, axis=-1, keepdims=True)
    o_ref[...] = (x * lax.rsqrt(ms + EPS) * g_ref[...]).astype(o_ref.dtype)


def _rmsnorm(x2d, g, out_dtype):
    m, d = x2d.shape
    tm = min(m, 512)
    return pl.pallas_call(
        _rmsnorm_kernel,
        grid=(m // tm,),
        in_specs=[pl.BlockSpec((tm, d), lambda i: (i, 0)), pl.BlockSpec((1, d), lambda i: (0, 0))],
        out_specs=pl.BlockSpec((tm, d), lambda i: (i, 0)),
        out_shape=jax.ShapeDtypeStruct((m, d), out_dtype),
        compiler_params=_cparams(("parallel",)),
        name="rmsnorm",
    )(x2d, g.reshape(1, d).astype(F32))


def _rope_tables(pos, batch):
    posf = pos.astype(F32)[:, None]
    half = HEAD_DIM // 2
    inv = ROPE_THETA ** (-jnp.arange(half, dtype=F32) / half)
    ang = posf * inv[None, :]
    c1, s1 = jnp.cos(ang), jnp.sin(ang)
    half2 = IDX_DIM // 2
    inv2 = ROPE_THETA ** (-jnp.arange(half2, dtype=F32) / half2)
    ang2 = posf * inv2[None, :]
    c2, s2 = jnp.cos(ang2), jnp.sin(ang2)
    tab = jnp.concatenate([c1, c1, -s1, s1, c2, c2, c2, c2, -s2, s2, -s2, s2], axis=1)
    return jnp.tile(tab, (batch, 1))


def _rope128(x, c, s):
    return x * c + pltpu.roll(x, 64, 1) * s


def _rope64(x, c, s):
    lane = _iota(x.shape, 1)
    part = jnp.where((lane & 32) == 0, pltpu.roll(x, 96, 1), pltpu.roll(x, 32, 1))
    return x * c + part * s


def _inproj_kernel(h_ref, w_ref, rt_ref, o_ref):
    j = pl.program_id(1)
    acc = jnp.dot(h_ref[...], w_ref[...], preferred_element_type=F32)
    nb = PROJ_TN // LANES

    @pl.when(j < 4)
    def _():
        c, s = rt_ref[:, 0:128], rt_ref[:, 128:256]
        for hb in range(nb):
            o_ref[:, hb * 128:(hb + 1) * 128] = _rope128(acc[:, hb * 128:(hb + 1) * 128], c, s)

    @pl.when((j >= 6) & (j < 8))
    def _():
        c, s = rt_ref[:, 256:384], rt_ref[:, 384:512]
        for hb in range(nb):
            o_ref[:, hb * 128:(hb + 1) * 128] = _rope64(acc[:, hb * 128:(hb + 1) * 128], c, s)

    @pl.when(j == 12)
    def _():
        c, s = rt_ref[:, 256:384], rt_ref[:, 384:512]
        o_ref[:, 0:128] = _rope64(acc[:, 0:128], c, s)
        o_ref[:, 128:] = acc[:, 128:]

    @pl.when(((j >= 4) & (j < 6)) | ((j >= 8) & (j < 12)))
    def _():
        o_ref[...] = acc


def _inproj(h, w, rt):
    m, d = h.shape
    tm = min(m, 1024)
    return pl.pallas_call(
        _inproj_kernel,
        grid=(m // tm, N_P // PROJ_TN),
        in_specs=[pl.BlockSpec((tm, d), lambda i, j: (i, 0)),
                  pl.BlockSpec((d, PROJ_TN), lambda i, j: (0, j)),
                  pl.BlockSpec((tm, 512), lambda i, j: (i, 0))],
        out_specs=pl.BlockSpec((tm, PROJ_TN), lambda i, j: (i, j)),
        out_shape=jax.ShapeDtypeStruct((m, N_P), F32),
        compiler_params=_cparams(("parallel", "arbitrary")),
        name="inproj",
    )(h, w, rt)


def _prep_w_in(w):
    parts, start = [], 0
    for sz in IN_SIZES:
        parts.append(w[:, start:start + sz])
        start += sz
    q, k, v, iq, ik, iw, z, xbc, dt, u = parts
    d = w.shape[0]
    small = jnp.concatenate([iw, dt, jnp.zeros((d, LANES - IDX_HEADS - M_HEADS), w.dtype)], axis=1)
    tail = jnp.zeros((d, N_P - C_SM - LANES), w.dtype)
    return jnp.concatenate([q, k, v, iq, xbc, z, u, ik, ik, small, tail], axis=1).astype(BF16)


def _sort_key(x):
    bits = pltpu.bitcast(x, I32)
    return jnp.where(bits < 0, bits ^ jnp.int32(0x7FFFFFFF), bits)


def _radix_select(count_fn, k_eff, rows, idx_bits):
    kf = k_eff.astype(F32)

    def bit_body(it, t):
        cand = t + (jnp.int32(1) << (31 - it))
        cnt = count_fn(lambda key, idx: jnp.where(key >= cand, 1.0, 0.0))
        return jnp.where(cnt >= kf, cand, t)

    t = lax.fori_loop(0, 32, bit_body, jnp.full((rows, 1), INT_MIN, I32))
    need = kf - count_fn(lambda key, idx: jnp.where(key > t, 1.0, 0.0))

    def tie_body(it, c):
        cand = c + (jnp.int32(1) << (idx_bits - 1 - it))
        cnt = count_fn(lambda key, idx: jnp.where(key == t, jnp.where(idx < cand, 1.0, 0.0), 0.0))
        return jnp.where(cnt < need, cand, c)

    c = lax.fori_loop(0, idx_bits, tie_body, jnp.zeros((rows, 1), I32))
    return t, c


def _selected(key, idx, t, c):
    return jnp.where(key > t, 1.0, jnp.where(key == t, jnp.where(idx <= c, 1.0, 0.0), 0.0))


def _pidx_kernel(iq_ref, ik_ref, sm_ref, mask_ref, key_sc, iqb_sc, *, tq, tk, topk, seq):
    i = pl.program_id(1)
    n_chunks = ((i + 1) * tq + tk - 1) // tk
    iqb_sc[...] = iq_ref[...].astype(BF16)
    w = sm_ref[:, SM_IW:SM_IW + IDX_HEADS] * ((IDX_HEADS ** -0.5) * (IDX_DIM ** -0.5))
    qpos = i * tq + _iota((tq, 1), 0)
    lane_k = _iota((tk, LANES), 1)
    mask_ref[...] = jnp.zeros(mask_ref.shape, mask_ref.dtype)

    def score_chunk(c, carry):
        off = pl.multiple_of(c * tk, tk)
        ik = ik_ref[pl.ds(off, tk), :]
        ik_e = jnp.where(lane_k < IDX_DIM, ik, 0.0).astype(BF16)
        ik_o = jnp.where(lane_k >= IDX_DIM, ik, 0.0).astype(BF16)
        acc = jnp.zeros((tq, tk), F32)
        for p in range(IDX_HEADS // 2):
            qp = iqb_sc[:, p * 128:(p + 1) * 128]
            acc = acc + w[:, 2 * p:2 * p + 1] * jnp.maximum(_nt_dot(qp, ik_e), 0.0)
            acc = acc + w[:, 2 * p + 1:2 * p + 2] * jnp.maximum(_nt_dot(qp, ik_o), 0.0)
        kpos = off + _iota((tq, tk), 1)
        acc = jnp.where(kpos <= qpos, acc, -jnp.inf)
        key_sc[:, pl.ds(off, tk)] = _sort_key(acc)
        return carry

    lax.fori_loop(0, n_chunks, score_chunk, 0)

    def count_fn(pred):
        def body(c, part):
            off = pl.multiple_of(c * tk, tk)
            key = key_sc[:, pl.ds(off, tk)]
            idx = off + _iota((tq, tk), 1)
            m = pred(key, idx)
            p = m[:, 0:LANES]
            for t in range(1, tk // LANES):
                p = p + m[:, t * LANES:(t + 1) * LANES]
            return part + p
        part = lax.fori_loop(0, n_chunks, body, jnp.zeros((tq, LANES), F32))
        return jnp.sum(part, axis=1, keepdims=True)

    k_eff = jnp.minimum(topk, qpos + 1)
    t, c = _radix_select(count_fn, k_eff, tq, max(1, (seq - 1).bit_length()))

    def write_chunk(cc, carry):
        off = pl.multiple_of(cc * tk, tk)
        key = key_sc[:, pl.ds(off, tk)]
        idx = off + _iota((tq, tk), 1)
        mask_ref[:, pl.ds(off, tk)] = _selected(key, idx, t, c).astype(mask_ref.dtype)
        return carry

    lax.fori_loop(0, n_chunks, write_chunk, 0)


def _prompt_index_mask(p3, topk):
    b, s, _ = p3.shape
    tq = min(s, 256)
    tk = min(s, 512)
    kern = functools.partial(_pidx_kernel, tq=tq, tk=tk, topk=topk, seq=s)
    return pl.pallas_call(
        kern,
        grid=(b, s // tq),
        in_specs=[pl.BlockSpec((None, tq, 1024), lambda bb, i: (bb, i, C_IQ // 1024)),
                  pl.BlockSpec((None, s, LANES), lambda bb, i: (bb, 0, C_IK // LANES)),
                  pl.BlockSpec((None, tq, LANES), lambda bb, i: (bb, i, C_SM // LANES))],
        out_specs=pl.BlockSpec((None, tq, s), lambda bb, i: (bb, i, 0)),
        out_shape=jax.ShapeDtypeStruct((b, s, s), BF16),
        scratch_shapes=[pltpu.VMEM((tq, s), I32), pltpu.VMEM((tq, 1024), BF16)],
        compiler_params=_cparams(("parallel", "arbitrary")),
        name="prompt_index_mask",
    )(p3, p3, p3)


def _pattn_kernel(q_ref, k_ref, v_ref, mk_ref, o_ref, m_sc, l_sc, acc_sc, *, tq, tk, nk):
    i = pl.program_id(1)
    kc = pl.program_id(2)
    last = ((i + 1) * tq - 1) // tk
    scale = HEAD_DIM ** -0.5
    rep = tk // LANES

    @pl.when(kc == 0)
    def _():
        m_sc[...] = jnp.full(m_sc.shape, NEG_BIG, F32)
        l_sc[...] = jnp.zeros(l_sc.shape, F32)
        acc_sc[...] = jnp.zeros(acc_sc.shape, F32)

    @pl.when(kc <= last)
    def _():
        mk = mk_ref[...] > 0
        for h in range(ATT_HEADS):
            sl = slice(h * HEAD_DIM, (h + 1) * HEAD_DIM)
            qh = q_ref[:, sl].astype(BF16)
            kh = k_ref[:, sl].astype(BF16)
            s = _nt_dot(qh, kh) * scale
            s = jnp.where(mk, s, NEG_BIG)
            m_old = m_sc[:, sl]
            m_new = jnp.maximum(m_old, jnp.max(s, axis=1, keepdims=True))
            alpha = jnp.exp(m_old - m_new)
            p = jnp.where(mk, jnp.exp(s - jnp.concatenate([m_new] * rep, axis=1)), 0.0)
            l_sc[:, sl] = alpha * l_sc[:, sl] + jnp.sum(p, axis=1, keepdims=True)
            pv = jnp.dot(p.astype(BF16), v_ref[:, sl].astype(BF16), preferred_element_type=F32)
            acc_sc[:, sl] = alpha * acc_sc[:, sl] + pv
            m_sc[:, sl] = m_new

    @pl.when(kc == nk - 1)
    def _():
        o_ref[...] = (acc_sc[...] / l_sc[...]).astype(o_ref.dtype)


def _prompt_attention(p3, mask):
    b, s, _ = p3.shape
    tq = min(s, 256)
    tk = min(s, 256)
    nk = s // tk

    def kv_map(col):
        return lambda bb, i, kc: (bb, jnp.minimum(kc, ((i + 1) * tq - 1) // tk), col)

    kern = functools.partial(_pattn_kernel, tq=tq, tk=tk, nk=nk)
    return pl.pallas_call(
        kern,
        grid=(b, s // tq, nk),
        in_specs=[pl.BlockSpec((None, tq, 1024), lambda bb, i, kc: (bb, i, C_Q // 1024)),
                  pl.BlockSpec((None, tk, 1024), kv_map(C_K // 1024)),
                  pl.BlockSpec((None, tk, 1024), kv_map(C_V // 1024)),
                  pl.BlockSpec((None, tq, tk),
                               lambda bb, i, kc: (bb, i, jnp.minimum(kc, ((i + 1) * tq - 1) // tk)))],
        out_specs=pl.BlockSpec((None, tq, 1024), lambda bb, i, kc: (bb, i, 0)),
        out_shape=jax.ShapeDtypeStruct((b, s, ATT_WIDTH), BF16),
        scratch_shapes=[pltpu.VMEM((tq, 1024), F32)] * 3,
        compiler_params=_cparams(("parallel", "parallel", "arbitrary")),
        name="prompt_attention",
    )(p3, p3, p3, mask)


def _sidx_kernel(pt_ref, iq_ref, w_ref, iknew_ref, *rest, pg, nsteps, past, n_new, topk):
    pages = rest[:pg]
    mask_ref = rest[pg]
    key_sc = rest[pg + 1]
    st = pl.program_id(1)
    iq = iq_ref[...]
    w = w_ref[...]
    width = past + LANES

    def score(ikp):
        d = jnp.maximum(_nt_dot(iq, ikp.astype(BF16)), 0.0) * w
        acc = d[0:SUBLANES]
        for h in range(1, IDX_HEADS):
            acc = acc + d[h * SUBLANES:(h + 1) * SUBLANES]
        return acc

    for r in range(pg):
        off = pl.multiple_of((st * pg + r) * PAGE_SIZE, PAGE_SIZE)
        key_sc[:, pl.ds(off, PAGE_SIZE)] = _sort_key(score(pages[r][...]))

    @pl.when(st == nsteps - 1)
    def _():
        trow = _iota((SUBLANES, LANES), 0)
        jcol = _iota((SUBLANES, LANES), 1)
        sc = score(iknew_ref[...])
        sc = jnp.where(jcol <= trow, jnp.where(jcol < n_new, sc, -jnp.inf), -jnp.inf)
        key_sc[:, past:width] = _sort_key(sc)
        key = key_sc[...]
        idx = _iota((SUBLANES, width), 1)

        def count_fn(pred):
            return jnp.sum(pred(key, idx), axis=1, keepdims=True)

        qpos = past + _iota((SUBLANES, 1), 0)
        k_eff = jnp.minimum(topk, qpos + 1)
        t, c = _radix_select(count_fn, k_eff, SUBLANES, max(1, (width - 1).bit_length()))
        mask_ref[...] = _selected(key, idx, t, c)


def _sample_index_mask(page_table, iq_rows, w_col, ik_new, pool_ik, layer, n_new, topk):
    b, npages = page_table.shape
    pg = 8 if npages % 8 == 0 else npages
    nsteps = npages // pg
    past = npages * PAGE_SIZE
    width = past + LANES

    def page_spec(r):
        return pl.BlockSpec((None, None, PAGE_SIZE, IDX_DIM),
                            lambda bb, st, pt: (layer, pt[bb, st * pg + r], 0, 0))

    kern = functools.partial(_sidx_kernel, pg=pg, nsteps=nsteps, past=past, n_new=n_new, topk=topk)
    grid_spec = pltpu.PrefetchScalarGridSpec(
        num_scalar_prefetch=1,
        grid=(b, nsteps),
        in_specs=[pl.BlockSpec((None, IDX_HEADS * SUBLANES, IDX_DIM), lambda bb, st, pt: (bb, 0, 0)),
                  pl.BlockSpec((None, IDX_HEADS * SUBLANES, 1), lambda bb, st, pt: (bb, 0, 0)),
                  pl.BlockSpec((None, LANES, IDX_DIM), lambda bb, st, pt: (bb, 0, 0))]
                 + [page_spec(r) for r in range(pg)],
        out_specs=pl.BlockSpec((None, SUBLANES, width), lambda bb, st, pt: (bb, 0, 0)),
        scratch_shapes=[pltpu.VMEM((SUBLANES, width), I32)],
    )
    return pl.pallas_call(
        kern,
        grid_spec=grid_spec,
        out_shape=jax.ShapeDtypeStruct((b, SUBLANES, width), F32),
        compiler_params=_cparams(("parallel", "arbitrary")),
        name="sample_index_mask",
    )(page_table, iq_rows, w_col, ik_new, *([pool_ik] * pg))


def _sattn_kernel(pt_ref, qbd_ref, mask_ref, knew_ref, vnew_ref, *rest, pg, nsteps, past):
    kpages = rest[:pg]
    vpages = rest[pg:2 * pg]
    o_ref = rest[2 * pg]
    m_sc, l_sc, acc_sc = rest[2 * pg + 1:]
    st = pl.program_id(1)
    scale = HEAD_DIM ** -0.5
    rows = ATT_HEADS * SUBLANES

    @pl.when(st == 0)
    def _():
        m_sc[...] = jnp.full(m_sc.shape, NEG_BIG, F32)
        l_sc[...] = jnp.zeros(l_sc.shape, F32)
        acc_sc[...] = jnp.zeros(acc_sc.shape, F32)

    qbd = qbd_ref[...]

    def update(kpage, vpage, mk8):
        mk = jnp.concatenate([mk8] * ATT_HEADS, axis=0) > 0
        s = _nt_dot(qbd, kpage.astype(BF16)) * scale
        s = jnp.where(mk, s, NEG_BIG)
        m_old = m_sc[...]
        m_new = jnp.maximum(m_old, jnp.max(s, axis=1, keepdims=True))
        alpha = jnp.exp(m_old - m_new)
        p = jnp.where(mk, jnp.exp(s - m_new), 0.0)
        l_sc[...] = alpha * l_sc[...] + jnp.sum(p, axis=1, keepdims=True)
        pv = jnp.dot(p.astype(BF16), vpage.astype(BF16), preferred_element_type=F32)
        acc_sc[...] = jnp.concatenate([alpha] * ATT_HEADS, axis=1) * acc_sc[...] + pv
        m_sc[...] = m_new

    for r in range(pg):
        off = pl.multiple_of((st * pg + r) * PAGE_SIZE, PAGE_SIZE)
        update(kpages[r][...], vpages[r][...], mask_ref[:, pl.ds(off, PAGE_SIZE)])

    @pl.when(st == nsteps - 1)
    def _():
        update(knew_ref[...], vnew_ref[...], mask_ref[:, past:past + LANES])
        for h in range(ATT_HEADS):
            rs = slice(h * SUBLANES, (h + 1) * SUBLANES)
            cs = slice(h * HEAD_DIM, (h + 1) * HEAD_DIM)
            o_ref[:, cs] = (acc_sc[rs, cs] / l_sc[rs, :]).astype(o_ref.dtype)


def _sample_attention(page_table, qbd, mask, k_new, v_new, pool_k, pool_v, layer):
    b, npages = page_table.shape
    pg = 8 if npages % 8 == 0 else npages
    nsteps = npages // pg
    past = npages * PAGE_SIZE
    width = past + LANES
    rows = ATT_HEADS * SUBLANES

    def page_spec(r):
        return pl.BlockSpec((None, None, PAGE_SIZE, ATT_WIDTH),
                            lambda bb, st, pt: (layer, pt[bb, st * pg + r], 0, 0))

    kern = functools.partial(_sattn_kernel, pg=pg, nsteps=nsteps, past=past)
    grid_spec = pltpu.PrefetchScalarGridSpec(
        num_scalar_prefetch=1,
        grid=(b, nsteps),
        in_specs=[pl.BlockSpec((None, rows, ATT_WIDTH), lambda bb, st, pt: (bb, 0, 0)),
                  pl.BlockSpec((None, SUBLANES, width), lambda bb, st, pt: (bb, 0, 0)),
                  pl.BlockSpec((None, LANES, ATT_WIDTH), lambda bb, st, pt: (bb, 0, 0)),
                  pl.BlockSpec((None, LANES, ATT_WIDTH), lambda bb, st, pt: (bb, 0, 0))]
                 + [page_spec(r) for r in range(pg)] * 2,
        out_specs=pl.BlockSpec((None, SUBLANES, ATT_WIDTH), lambda bb, st, pt: (bb, 0, 0)),
        scratch_shapes=[pltpu.VMEM((rows, LANES), F32), pltpu.VMEM((rows, LANES), F32),
                        pltpu.VMEM((rows, ATT_WIDTH), F32)],
    )
    return pl.pallas_call(
        kern,
        grid_spec=grid_spec,
        out_shape=jax.ShapeDtypeStruct((b, SUBLANES, ATT_WIDTH), BF16),
        compiler_params=_cparams(("parallel", "arbitrary")),
        name="sample_attention",
    )(page_table, qbd, mask, k_new, v_new, *([pool_k] * pg), *([pool_v] * pg))


def _shift_rows(x, prev, d):
    r = pltpu.roll(x, d, 0)
    rowi = _iota(x.shape, 0)
    np_ = prev.shape[0]
    for q in range(d):
        r = jnp.where(rowi == q, prev[np_ - d + q:np_ - d + q + 1, :], r)
    return r


def _per_head(v):
    lane = _iota((v.shape[0], LANES), 1)
    outs = []
    for j in range(M_HEADS // 2):
        a = v[:, SM_DT + 2 * j:SM_DT + 2 * j + 1]
        b = v[:, SM_DT + 2 * j + 1:SM_DT + 2 * j + 2]
        outs.append(jnp.where(lane < M_HEAD_DIM, a, b))
    return jnp.concatenate(outs, axis=1)


def _mamba_kernel(xbc_ref, halo_ref, z_ref, sm_ref, cprev_ref, h0_ref, cw_ref, cb_ref, hp_ref,
                  dexp_ref, ng_ref, y_ref, hout_ref, h_sc, *, rows, n_valid, nsteps):
    c = pl.program_id(1)
    L = SSD_L
    hi = lax.Precision.HIGHEST

    @pl.when(c == 0)
    def _():
        h_sc[...] = h0_ref[...]

    x = xbc_ref[...]
    prev = jnp.where(c == 0, cprev_ref[...], halo_ref[SUBLANES - (M_CONV - 1):SUBLANES, :])
    conv = cb_ref[...] + x * cw_ref[M_CONV - 1:M_CONV, :]
    for d in range(1, M_CONV):
        conv = conv + _shift_rows(x, prev, d) * cw_ref[M_CONV - 1 - d:M_CONV - d, :]
    xc = conv * _sigmoid(conv)

    smv = sm_ref[...] + hp_ref[0:1, :]
    dt = jnp.maximum(smv, 0.0) + jnp.log1p(jnp.exp(-jnp.abs(smv)))
    rowg = c * rows + _iota((rows, LANES), 0)
    dt = jnp.where(rowg < n_valid, dt, 0.0)
    zz = z_ref[...]
    if rows < L:
        xc = jnp.concatenate([xc, jnp.zeros((L - rows, xc.shape[1]), F32)], axis=0)
        dt = jnp.concatenate([dt, jnp.zeros((L - rows, LANES), F32)], axis=0)
        zz = jnp.concatenate([zz, jnp.zeros((L - rows, zz.shape[1]), F32)], axis=0)

    xs = xc[:, 0:M_WIDTH]
    bm = xc[:, M_WIDTH:M_WIDTH + M_GROUPS * M_STATE].astype(BF16)
    cm = xc[:, M_WIDTH + M_GROUPS * M_STATE:].astype(BF16)

    da = dt * (-jnp.exp(hp_ref[1:2, :]))
    ri = _iota((L, L), 0)
    ci = _iota((L, L), 1)
    causal = ri >= ci
    acum = jnp.dot(jnp.where(causal, 1.0, 0.0), da, precision=hi, preferred_element_type=F32)
    acum_t = acum.T
    alast = acum[L - 1:L, :]

    xdt = xs * _per_head(dt)
    xw = xdt * _per_head(jnp.exp(alast - acum))
    xw_t = jnp.concatenate([xw[:, j * LANES:(j + 1) * LANES].T for j in range(M_WIDTH // LANES)], axis=0)
    h_in = h_sc[...]
    gw = M_WIDTH // M_GROUPS

    y_off, states, cb = [], [], []
    for g in range(M_GROUPS):
        bg = bm[:, g * M_STATE:(g + 1) * M_STATE]
        cg = cm[:, g * M_STATE:(g + 1) * M_STATE]
        states.append(jnp.dot(xw_t[g * gw:(g + 1) * gw, :].astype(BF16), bg, preferred_element_type=F32))
        y_off.append(_nt_dot(cg, h_in[g * gw:(g + 1) * gw, :].astype(BF16)))
        cb.append(_nt_dot(cg, bg))
    y = jnp.concatenate(y_off, axis=1) * _per_head(jnp.exp(acum))

    lane = _iota((L, LANES), 1)
    y_diag = []
    for j in range(M_HEADS // 2):
        xpair = xdt[:, j * LANES:(j + 1) * LANES]
        acc = jnp.zeros((L, LANES), F32)
        for e in range(2):
            h = 2 * j + e
            seg = acum[:, SM_DT + h:SM_DT + h + 1] - acum_t[SM_DT + h:SM_DT + h + 1, :]
            mh = (cb[h // (M_HEADS // M_GROUPS)] * jnp.exp(jnp.where(causal, seg, -jnp.inf))).astype(BF16)
            keep = (lane < M_HEAD_DIM) if e == 0 else (lane >= M_HEAD_DIM)
            acc = acc + jnp.dot(mh, jnp.where(keep, xpair, 0.0).astype(BF16), preferred_element_type=F32)
        y_diag.append(acc)
    y = y + jnp.concatenate(y_diag, axis=1) + xs * dexp_ref[...]

    cdec = jnp.concatenate(
        [jnp.broadcast_to(acum_t[SM_DT + h:SM_DT + h + 1, L - 1:L], (M_HEAD_DIM, M_STATE))
         for h in range(M_HEADS)], axis=0)
    h_new = h_in * jnp.exp(cdec) + jnp.concatenate(states, axis=0)
    h_sc[...] = h_new

    @pl.when(c == nsteps - 1)
    def _():
        hout_ref[...] = h_new

    y = y * (zz * _sigmoid(zz))
    outs = []
    for g in range(M_GROUPS):
        seg = y[:, g * gw:(g + 1) * gw]
        ms = jnp.mean(seg * seg, axis=-1, keepdims=True)
        outs.append(seg * lax.rsqrt(ms + EPS) * ng_ref[:, g * gw:(g + 1) * gw])
    y_ref[...] = jnp.concatenate(outs, axis=1)[0:rows].astype(y_ref.dtype)


def _mamba(p3, conv_prev, h0, prm, n_valid):
    b, tp, _ = p3.shape
    rows = min(tp, SSD_L)
    nsteps = tp // rows
    hb = rows // SUBLANES
    kern = functools.partial(_mamba_kernel, rows=rows, n_valid=n_valid, nsteps=nsteps)
    full2 = lambda shape: pl.BlockSpec(shape, lambda bb, c: (0, 0))
    y, hout = pl.pallas_call(
        kern,
        grid=(b, nsteps),
        in_specs=[pl.BlockSpec((None, rows, M_CONV_CH), lambda bb, c: (bb, c, C_XBC // M_CONV_CH)),
                  pl.BlockSpec((None, SUBLANES, M_CONV_CH),
                               lambda bb, c: (bb, jnp.maximum(c * hb - 1, 0), C_XBC // M_CONV_CH)),
                  pl.BlockSpec((None, rows, M_WIDTH), lambda bb, c: (bb, c, C_Z // M_WIDTH)),
                  pl.BlockSpec((None, rows, LANES), lambda bb, c: (bb, c, C_SM // LANES)),
                  pl.BlockSpec((None, M_CONV - 1, M_CONV_CH), lambda bb, c: (bb, 0, 0)),
                  pl.BlockSpec((None, M_WIDTH, M_STATE), lambda bb, c: (bb, 0, 0)),
                  full2((M_CONV, M_CONV_CH)), full2((1, M_CONV_CH)), full2((2, LANES)),
                  full2((1, M_WIDTH)), full2((1, M_WIDTH))],
        out_specs=[pl.BlockSpec((None, rows, M_WIDTH), lambda bb, c: (bb, c, 0)),
                   pl.BlockSpec((None, M_WIDTH, M_STATE), lambda bb, c: (bb, 0, 0))],
        out_shape=[jax.ShapeDtypeStruct((b, tp, M_WIDTH), BF16),
                   jax.ShapeDtypeStruct((b, M_WIDTH, M_STATE), F32)],
        scratch_shapes=[pltpu.VMEM((M_WIDTH, M_STATE), F32)],
        compiler_params=_cparams(("parallel", "arbitrary")),
        name="mamba",
    )(p3, p3, p3, p3, conv_prev.astype(F32), h0.reshape(b, M_WIDTH, M_STATE).astype(F32),
      prm["m_conv_w"], prm["m_conv_b"], prm["m_hp"], prm["m_dexp"], prm["m_ng"])
    return y, hout.reshape(b, M_HEADS, M_HEAD_DIM, M_STATE)


def _gelu_tanh(x):
    return 0.5 * x * (1.0 + jnp.tanh(math.sqrt(2.0 / math.pi) * (x + 0.044715 * (x * x * x))))


def _s5_kernel(u_ref, h0r_ref, h0i_ref, lam_ref, bre_ref, bim_ref, cre_ref, cim_ref, dg_ref, gw_ref,
               o_ref, sr_ref, si_ref, cr_sc, ci_sc, *, rows, n_valid):
    t = pl.program_id(1)

    @pl.when(t == 0)
    def _():
        cr_sc[...] = h0r_ref[...]
        ci_sc[...] = h0i_ref[...]

    lr, li = lam_ref[0:1, :], lam_ref[1:2, :]
    step = jnp.exp(lam_ref[2:3, :])
    mag = jnp.exp(lr * step)
    ar, ai = mag * jnp.cos(li * step), mag * jnp.sin(li * step)
    den = lr * lr + li * li
    gr = ((ar - 1.0) * lr + ai * li) / den
    gi = (ai * lr - (ar - 1.0) * li) / den

    u = u_ref[...]
    ub = u.astype(BF16)
    bur = jnp.dot(ub, bre_ref[...], preferred_element_type=F32)
    bui = jnp.dot(ub, bim_ref[...], preferred_element_type=F32)
    row = _iota((rows, S5_N), 0)
    c_r, c_i = cr_sc[...], ci_sc[...]
    hr = gr * bur - gi * bui + jnp.where(row == 0, ar * c_r - ai * c_i, 0.0)
    hi = gr * bui + gi * bur + jnp.where(row == 0, ar * c_i + ai * c_r, 0.0)

    pr, pi = ar, ai
    d = 1
    while d < rows:
        sr = jnp.where(row >= d, pltpu.roll(hr, d, 0), 0.0)
        si = jnp.where(row >= d, pltpu.roll(hi, d, 0), 0.0)
        hr, hi = hr + pr * sr - pi * si, hi + pr * si + pi * sr
        pr, pi = pr * pr - pi * pi, 2.0 * pr * pi
        d *= 2

    cr_sc[...] = hr[rows - 1:rows, :]
    ci_sc[...] = hi[rows - 1:rows, :]
    r_last = (n_valid - 1) % rows

    @pl.when(t == (n_valid - 1) // rows)
    def _():
        sr_ref[...] = hr[r_last:r_last + 1, :]
        si_ref[...] = hi[r_last:r_last + 1, :]

    y = (jnp.dot(hr.astype(BF16), cre_ref[...], preferred_element_type=F32)
         - jnp.dot(hi.astype(BF16), cim_ref[...], preferred_element_type=F32)
         + dg_ref[0:1, :] * u)
    g = _gelu_tanh(y)
    gate = jnp.dot(g.astype(BF16), gw_ref[...], preferred_element_type=F32) + dg_ref[1:2, :]
    o_ref[...] = (g * _sigmoid(gate)).astype(o_ref.dtype)


def _s5(p3, h0_re, h0_im, prm, n_valid):
    b, tp, _ = p3.shape
    rows = min(tp, 256)
    nsteps = tp // rows
    kern = functools.partial(_s5_kernel, rows=rows, n_valid=n_valid)
    full2 = lambda shape: pl.BlockSpec(shape, lambda bb, t: (0, 0))
    state_spec = pl.BlockSpec((None, 1, S5_N), lambda bb, t: (bb, 0, 0))
    out, sr, si = pl.pallas_call(
        kern,
        grid=(b, nsteps),
        in_specs=[pl.BlockSpec((None, rows, S5_WIDTH), lambda bb, t: (bb, t, C_U // S5_WIDTH)),
                  state_spec, state_spec, full2((3, S5_N)),
                  full2((S5_WIDTH, S5_N)), full2((S5_WIDTH, S5_N)),
                  full2((S5_N, S5_WIDTH)), full2((S5_N, S5_WIDTH)),
                  full2((2, S5_WIDTH)), full2((S5_WIDTH, S5_WIDTH))],
        out_specs=[pl.BlockSpec((None, rows, S5_WIDTH), lambda bb, t: (bb, t, 0)), state_spec, state_spec],
        out_shape=[jax.ShapeDtypeStruct((b, tp, S5_WIDTH), BF16),
                   jax.ShapeDtypeStruct((b, 1, S5_N), F32),
                   jax.ShapeDtypeStruct((b, 1, S5_N), F32)],
        scratch_shapes=[pltpu.VMEM((1, S5_N), F32), pltpu.VMEM((1, S5_N), F32)],
        compiler_params=_cparams(("parallel", "arbitrary")),
        name="s5",
    )(p3, h0_re.reshape(b, 1, S5_N).astype(F32), h0_im.reshape(b, 1, S5_N).astype(F32),
      prm["s5_lam"], prm["s5_bre"], prm["s5_bim"], prm["s5_cre"], prm["s5_cim"], prm["s5_dg"], prm["s5_gw"])
    return out, sr.reshape(b, S5_GROUPS, S5_STATE), si.reshape(b, S5_GROUPS, S5_STATE)


def _block_diag_in(w):
    g, n, k = w.shape
    eye = jnp.eye(g, dtype=w.dtype)
    return jnp.einsum('gnk,gh->gkhn', w, eye).reshape(g * k, g * n)


def _block_diag_out(w):
    g, k, n = w.shape
    eye = jnp.eye(g, dtype=w.dtype)
    return jnp.einsum('gkn,gh->gnhk', w, eye).reshape(g * n, g * k)


def _outproj_kernel(att_ref, mam_ref, s5_ref, w_ref, res_ref, o_ref):
    acc = jnp.dot(att_ref[...], w_ref[0:ATT_WIDTH, :], preferred_element_type=F32)
    acc = acc + jnp.dot(mam_ref[...], w_ref[ATT_WIDTH:ATT_WIDTH + M_WIDTH, :], preferred_element_type=F32)
    acc = acc + jnp.dot(s5_ref[...], w_ref[ATT_WIDTH + M_WIDTH:, :], preferred_element_type=F32)
    o_ref[...] = res_ref[...] + acc


def _outproj(att, mam, s5o, w, res):
    m = att.shape[0]
    tm = min(m, 1024)
    tn = 512
    return pl.pallas_call(
        _outproj_kernel,
        grid=(m // tm, D_MODEL // tn),
        in_specs=[pl.BlockSpec((tm, ATT_WIDTH), lambda i, j: (i, 0)),
                  pl.BlockSpec((tm, M_WIDTH), lambda i, j: (i, 0)),
                  pl.BlockSpec((tm, S5_WIDTH), lambda i, j: (i, 0)),
                  pl.BlockSpec((D_MODEL, tn), lambda i, j: (0, j)),
                  pl.BlockSpec((tm, tn), lambda i, j: (i, j))],
        out_specs=pl.BlockSpec((tm, tn), lambda i, j: (i, j)),
        out_shape=jax.ShapeDtypeStruct((m, D_MODEL), F32),
        compiler_params=_cparams(("parallel", "arbitrary")),
        name="outproj",
    )(att, mam, s5o, w, res)


def _matmul_kernel(a_ref, w_ref, o_ref):
    o_ref[...] = jnp.dot(a_ref[...], w_ref[...], preferred_element_type=F32).astype(o_ref.dtype)


def _matmul(a, w, tn=512):
    m, k = a.shape
    n = w.shape[1]
    tm = min(m, 1024)
    return pl.pallas_call(
        _matmul_kernel,
        grid=(m // tm, n // tn),
        in_specs=[pl.BlockSpec((tm, k), lambda i, j: (i, 0)), pl.BlockSpec((k, tn), lambda i, j: (0, j))],
        out_specs=pl.BlockSpec((tm, tn), lambda i, j: (i, j)),
        out_shape=jax.ShapeDtypeStruct((m, n), F32),
        compiler_params=_cparams(("parallel", "arbitrary")),
        name="ffn_up",
    )(a, w)


def _ffn_act_kernel(g_ref, v_ref, gh_ref, vh_ref, gp_ref, vp_ref, gw_ref, vw_ref, gb_ref, vb_ref, o_ref):
    i = pl.program_id(1)

    def conv(x_ref, halo_ref, prev_ref, w_ref, b_ref):
        x = x_ref[...]
        prev = jnp.where(i == 0, prev_ref[...], halo_ref[SUBLANES - (FFN_CONV - 1):SUBLANES, :])
        y = b_ref[...] + x * w_ref[FFN_CONV - 1:FFN_CONV, :]
        for d in range(1, FFN_CONV):
            y = y + _shift_rows(x, prev, d) * w_ref[FFN_CONV - 1 - d:FFN_CONV - d, :]
        return y

    gate = conv(g_ref, gh_ref, gp_ref, gw_ref, gb_ref)
    val = conv(v_ref, vh_ref, vp_ref, vw_ref, vb_ref)
    o_ref[...] = (gate * _sigmoid(gate) * val).astype(o_ref.dtype)


def _ffn_act(up3, prev, conv_w, conv_b):
    b, tp, _ = up3.shape
    tr = min(tp, 512)
    tc = 512
    nj = D_FF // tc
    hb = tr // SUBLANES
    main = lambda o: pl.BlockSpec((None, tr, tc), lambda bb, i, j: (bb, i, j + o))
    halo = lambda o: pl.BlockSpec((None, SUBLANES, tc), lambda bb, i, j: (bb, jnp.maximum(i * hb - 1, 0), j + o))
    prv = lambda o: pl.BlockSpec((None, FFN_CONV - 1, tc), lambda bb, i, j: (bb, 0, j + o))
    wsp = lambda o: pl.BlockSpec((FFN_CONV, tc), lambda bb, i, j: (0, j + o))
    bsp = lambda o: pl.BlockSpec((1, tc), lambda bb, i, j: (0, j + o))
    prev = prev.astype(F32)
    return pl.pallas_call(
        _ffn_act_kernel,
        grid=(b, tp // tr, nj),
        in_specs=[main(0), main(nj), halo(0), halo(nj), prv(0), prv(nj), wsp(0), wsp(nj), bsp(0), bsp(nj)],
        out_specs=pl.BlockSpec((None, tr, tc), lambda bb, i, j: (bb, i, j)),
        out_shape=jax.ShapeDtypeStruct((b, tp, D_FF), BF16),
        compiler_params=_cparams(("parallel", "parallel", "parallel")),
        name="ffn_act",
    )(up3, up3, up3, up3, prev, prev, conv_w, conv_w, conv_b, conv_b)


def _down_kernel(a_ref, w_ref, res_ref, o_ref):
    o_ref[...] = res_ref[...] + jnp.dot(a_ref[...], w_ref[...], preferred_element_type=F32)


def _ffn_down(act, w, res):
    m, k = act.shape
    tm = min(m, 512)
    tn = 512
    return pl.pallas_call(
        _down_kernel,
        grid=(m // tm, D_MODEL // tn),
        in_specs=[pl.BlockSpec((tm, k), lambda i, j: (i, 0)),
                  pl.BlockSpec((k, tn), lambda i, j: (0, j)),
                  pl.BlockSpec((tm, tn), lambda i, j: (i, j))],
        out_specs=pl.BlockSpec((tm, tn), lambda i, j: (i, j)),
        out_shape=jax.ShapeDtypeStruct((m, D_MODEL), F32),
        compiler_params=_cparams(("parallel", "arbitrary")),
        name="ffn_down",
    )(act, w, res)


def _prep_layer_params(l, w_in, m_conv_w, m_conv_b, m_dt_bias, m_a_log, m_d, m_norm_g,
                       s5_lam_re, s5_lam_im, s5_log_step, s5_b_re, s5_b_im, s5_c_re, s5_c_im,
                       s5_d, s5_glu_w, s5_glu_b, w_out, ffn_w_up, ffn_conv_w, ffn_conv_b, ffn_w_down):
    def lanes(v):
        return jnp.zeros((LANES,), F32).at[SM_DT:SM_DT + M_HEADS].set(v.astype(F32))

    return {
        "w_in": _prep_w_in(w_in[l]),
        "m_conv_w": m_conv_w[l].astype(F32),
        "m_conv_b": m_conv_b[l].reshape(1, M_CONV_CH).astype(F32),
        "m_hp": jnp.stack([lanes(m_dt_bias[l]), lanes(m_a_log[l])], axis=0),
        "m_dexp": jnp.repeat(m_d[l].astype(F32), M_HEAD_DIM).reshape(1, M_WIDTH),
        "m_ng": m_norm_g[l].reshape(1, M_WIDTH).astype(F32),
        "s5_lam": jnp.stack([s5_lam_re[l].reshape(S5_N), s5_lam_im[l].reshape(S5_N),
                             jnp.repeat(s5_log_step[l], S5_STATE)], axis=0).astype(F32),
        "s5_bre": _block_diag_in(s5_b_re[l]).astype(BF16),
        "s5_bim": _block_diag_in(s5_b_im[l]).astype(BF16),
        "s5_cre": _block_diag_out(s5_c_re[l]).astype(BF16),
        "s5_cim": _block_diag_out(s5_c_im[l]).astype(BF16),
        "s5_dg": jnp.stack([s5_d[l], s5_glu_b[l]], axis=0).astype(F32),
        "s5_gw": s5_glu_w[l].astype(BF16),
        "w_out": w_out[l].astype(BF16),
        "ffn_w_up": ffn_w_up[l].astype(BF16),
        "ffn_conv_w": ffn_conv_w[l].astype(F32),
        "ffn_conv_b": ffn_conv_b[l].reshape(1, 2 * D_FF).astype(F32),
        "ffn_w_down": ffn_w_down[l].astype(BF16),
    }


def _layer(x3, rt, attend, mconv_prev, ssm_h0, s5_h0_re, s5_h0_im, fconv_prev, norm1_g, norm2_g, prm, n_valid):
    b, tp, d = x3.shape
    m = b * tp
    x2 = x3.reshape(m, d)
    h = _rmsnorm(x2, norm1_g, BF16)
    p2 = _inproj(h, prm["w_in"], rt)
    p3 = p2.reshape(b, tp, N_P)
    att = attend(p3)
    mam, ssm_new = _mamba(p3, mconv_prev, ssm_h0, prm, n_valid)
    s5o, s5_re, s5_im = _s5(p3, s5_h0_re, s5_h0_im, prm, n_valid)
    x2 = _outproj(att.reshape(m, ATT_WIDTH), mam.reshape(m, M_WIDTH), s5o.reshape(m, S5_WIDTH), prm["w_out"], x2)
    h2 = _rmsnorm(x2, norm2_g, BF16)
    up = _matmul(h2, prm["ffn_w_up"])
    up3 = up.reshape(b, tp, 2 * D_FF)
    act = _ffn_act(up3, fconv_prev, prm["ffn_conv_w"], prm["ffn_conv_b"])
    x2 = _ffn_down(act.reshape(m, D_FF), prm["ffn_w_down"], x2)

    k = p3[:, :n_valid, C_K:C_K + ATT_WIDTH].reshape(b, n_valid, ATT_HEADS, HEAD_DIM)
    v = p3[:, :n_valid, C_V:C_V + ATT_WIDTH].reshape(b, n_valid, ATT_HEADS, HEAD_DIM)
    ik = p3[:, :n_valid, C_IK:C_IK + IDX_DIM]
    xbc_raw = jnp.concatenate([mconv_prev.astype(F32), p3[:, :n_valid, C_XBC:C_XBC + M_CONV_CH]], axis=1)
    mconv_new = xbc_raw[:, n_valid:]
    up_raw = jnp.concatenate([fconv_prev.astype(F32), up3[:, :n_valid]], axis=1)
    fconv_new = up_raw[:, n_valid:]
    return x2.reshape(b, tp, d), (k, v, ik, ssm_new, mconv_new, s5_re, s5_im, fconv_new)


def kernel(x_prompt, x_sample, cache_k, cache_v, cache_idx_k, state_ssm, state_mconv, state_s5_re, state_s5_im,
           state_fconv, page_table, norm1_g, w_in, m_conv_w, m_conv_b, m_dt_bias, m_a_log, m_d, m_norm_g,
           s5_lam_re, s5_lam_im, s5_log_step, s5_b_re, s5_b_im, s5_c_re, s5_c_im, s5_d, s5_glu_w, s5_glu_b,
           w_out, norm2_g, ffn_w_up, ffn_conv_w, ffn_conv_b, ffn_w_down, final_norm_g):
    bp, seq, d = x_prompt.shape
    bs, t_new, _ = x_sample.shape
    depth = w_in.shape[0]
    npages = page_table.shape[1]
    past = npages * PAGE_SIZE
    tpad = SUBLANES
    assert t_new <= tpad and seq % SSD_L == 0

    rt_p = _rope_tables(jnp.arange(seq, dtype=I32), bp)
    rt_s = _rope_tables(past + jnp.arange(tpad, dtype=I32), bs)
    hp = x_prompt.astype(F32)
    hs = jnp.pad(x_sample.astype(F32), ((0, 0), (0, tpad - t_new), (0, 0)))
    n_pool = cache_k.shape[1]
    pool_k = cache_k.reshape(depth, n_pool, PAGE_SIZE, ATT_WIDTH)
    pool_v = cache_v.reshape(depth, n_pool, PAGE_SIZE, ATT_WIDTH)
    topk_p = min(TOPK_MAX, seq // 4)
    topk_s = min(TOPK_MAX, (past + t_new) // 4)

    zero_mconv = jnp.zeros((bp, M_CONV - 1, M_CONV_CH), F32)
    zero_ssm = jnp.zeros((bp, M_HEADS, M_HEAD_DIM, M_STATE), F32)
    zero_s5 = jnp.zeros((bp, S5_GROUPS, S5_STATE), F32)
    zero_fconv = jnp.zeros((bp, FFN_CONV - 1, 2 * D_FF), F32)

    def attend_prompt(p3):
        return _prompt_attention(p3, _prompt_index_mask(p3, topk_p))

    new_p, new_s = [], []
    for l in range(depth):
        prm = _prep_layer_params(l, w_in, m_conv_w, m_conv_b, m_dt_bias, m_a_log, m_d, m_norm_g,
                                 s5_lam_re, s5_lam_im, s5_log_step, s5_b_re, s5_b_im, s5_c_re, s5_c_im,
                                 s5_d, s5_glu_w, s5_glu_b, w_out, ffn_w_up, ffn_conv_w, ffn_conv_b, ffn_w_down)

        def attend_sample(p3, l=l):
            iq = p3[:, :, C_IQ:C_IQ + IDX_HEADS * IDX_DIM].reshape(bs, tpad, IDX_HEADS, IDX_DIM)
            iq_rows = iq.transpose(0, 2, 1, 3).reshape(bs, IDX_HEADS * tpad, IDX_DIM).astype(BF16)
            iw = p3[:, :, C_SM + SM_IW:C_SM + SM_IW + IDX_HEADS] * ((IDX_HEADS ** -0.5) * (IDX_DIM ** -0.5))
            w_col = iw.transpose(0, 2, 1).reshape(bs, IDX_HEADS * tpad, 1)
            pad_rows = ((0, 0), (0, LANES - tpad), (0, 0))
            ik_new = jnp.pad(p3[:, :, C_IK:C_IK + IDX_DIM], pad_rows)
            mask = _sample_index_mask(page_table, iq_rows, w_col, ik_new, cache_idx_k, l, t_new, topk_s)
            q = p3[:, :, C_Q:C_Q + ATT_WIDTH].reshape(bs, tpad, ATT_HEADS, HEAD_DIM).transpose(0, 2, 1, 3)
            eye = jnp.eye(ATT_HEADS, dtype=F32)
            qbd = (q[:, :, :, None, :] * eye[None, :, None, :, None]).reshape(bs, ATT_HEADS * tpad, ATT_WIDTH)
            k_new = jnp.pad(p3[:, :, C_K:C_K + ATT_WIDTH], pad_rows)
            v_new = jnp.pad(p3[:, :, C_V:C_V + ATT_WIDTH], pad_rows)
            return _sample_attention(page_table, qbd.astype(BF16), mask, k_new, v_new, pool_k, pool_v, l)

        hp, st_p = _layer(hp, rt_p, attend_prompt, zero_mconv, zero_ssm, zero_s5, zero_s5, zero_fconv,
                          norm1_g[l], norm2_g[l], prm, seq)
        hs, st_s = _layer(hs, rt_s, attend_sample, state_mconv[l], state_ssm[l], state_s5_re[l], state_s5_im[l],
                          state_fconv[l], norm1_g[l], norm2_g[l], prm, t_new)
        new_p.append(st_p)
        new_s.append(st_s)

    def stacked(states, i):
        return jnp.stack([s[i] for s in states], axis=0)

    y_prompt = _rmsnorm(hp.reshape(bp * seq, d), final_norm_g, F32).reshape(bp, seq, d)
    y_sample = _rmsnorm(hs.reshape(bs * tpad, d), final_norm_g, F32).reshape(bs, tpad, d)[:, :t_new]
    return (y_prompt, y_sample,
            stacked(new_p, 0), stacked(new_p, 1), stacked(new_p, 2), stacked(new_p, 3),
            stacked(new_p, 4), stacked(new_p, 5), stacked(new_p, 6), stacked(new_p, 7),
            stacked(new_s, 0), stacked(new_s, 1), stacked(new_s, 2), stacked(new_s, 3),
            stacked(new_s, 4), stacked(new_s, 5), stacked(new_s, 6), stacked(new_s, 7))
```

```python
import functools
import math

import numpy as np
import jax
import jax.numpy as jnp
from jax import lax
from jax.experimental import pallas as pl
from jax.experimental.pallas import tpu as pltpu

F32 = jnp.float32
BF16 = jnp.bfloat16
I32 = jnp.int32

D_MODEL = 2048
HEAD_DIM = 128
ATT_WIDTH = D_MODEL // 2
ATT_HEADS = ATT_WIDTH // HEAD_DIM
IDX_HEADS = 16
IDX_DIM = 64
TOPK_MAX = 256
ROPE_THETA = 10000.0
PAGE_SIZE = 128
M_WIDTH = D_MODEL // 4
M_HEAD_DIM = 64
M_HEADS = M_WIDTH // M_HEAD_DIM
M_GROUPS = 2
M_STATE = 128
M_CONV = 4
M_CONV_CH = M_WIDTH + 2 * M_GROUPS * M_STATE
S5_WIDTH = D_MODEL - ATT_WIDTH - M_WIDTH
S5_GROUP = 16
S5_GROUPS = S5_WIDTH // S5_GROUP
S5_STATE = 64
S5_N = S5_GROUPS * S5_STATE
D_FF = (D_MODEL * 11) // 4
FFN_CONV = 3
EPS = 1e-6
IN_SIZES = (ATT_WIDTH, ATT_WIDTH, ATT_WIDTH, IDX_HEADS * IDX_DIM, IDX_DIM, IDX_HEADS,
            M_WIDTH, M_CONV_CH, M_HEADS, S5_WIDTH)

LANES = 128
SUBLANES = 8
VMEM_LIMIT = 48 * 1024 * 1024

N_QKV = 3 * ATT_WIDTH
C_Q, C_K, C_V = 0, 1024, 2048
C_IQ, C_XBC, C_Z, C_U, C_IK, C_SM = 0, 1024, 2048, 2560, 3072, 3200
N_P = 3584
PROJ_TN = 512
NJ_QKV = N_QKV // PROJ_TN
SM_IW = 0
SM_DT = 16
SSD_L = 128
FFN_TM = 1024
COUNT_ACC = 4
ATT_ROWS = 128
NEG_BIG = -1e30
INT_MIN = -2 ** 31
LOG2E = math.log2(math.e)
Q_SCALE = (HEAD_DIM ** -0.5) * LOG2E
ATT_KT = 256


def _cparams(sem):
    return pltpu.CompilerParams(dimension_semantics=sem, vmem_limit_bytes=VMEM_LIMIT)


def _iota(shape, dim):
    return lax.broadcasted_iota(I32, shape, dim)


def _sigmoid(x):
    return 1.0 / (1.0 + jnp.exp(-x))


def _nt_dot(a, b):
    return lax.dot_general(a, b, (((1,), (1,)), ((), ())), preferred_element_type=F32)


def _rmsnorm_kernel(x_ref, g_ref, o_ref):
    x = x_ref[...]
    ms = jnp.mean(x * x, axis=-1, keepdims=True)
    o_ref[...] = (x * lax.rsqrt(ms + EPS) * g_ref[...]).astype(o_ref.dtype)


def _rmsnorm(x2d, g, out_dtype):
    m, d = x2d.shape
    tm = min(m, 512)
    return pl.pallas_call(
        _rmsnorm_kernel,
        grid=(m // tm,),
        in_specs=[pl.BlockSpec((tm, d), lambda i: (i, 0)), pl.BlockSpec((1, d), lambda i: (0, 0))],
        out_specs=pl.BlockSpec((tm, d), lambda i: (i, 0)),
        out_shape=jax.ShapeDtypeStruct((m, d), out_dtype),
        compiler_params=_cparams(("parallel",)),
        name="rmsnorm",
    )(x2d, g.reshape(1, d).astype(F32))


def _rope_tables(pos, batch):
    posf = pos.astype(F32)[:, None]
    half = HEAD_DIM // 2
    inv = ROPE_THETA ** (-jnp.arange(half, dtype=F32) / half)
    ang = posf * inv[None, :]
    c1, s1 = jnp.cos(ang), jnp.sin(ang)
    half2 = IDX_DIM // 2
    inv2 = ROPE_THETA ** (-jnp.arange(half2, dtype=F32) / half2)
    ang2 = posf * inv2[None, :]
    c2, s2 = jnp.cos(ang2), jnp.sin(ang2)
    tab = jnp.concatenate([c1, c1, -s1, s1, c2, c2, c2, c2, -s2, s2, -s2, s2], axis=1)
    return jnp.tile(tab, (batch, 1))


def _rope128(x, c, s):
    return x * c + pltpu.roll(x, 64, 1) * s


def _rope64(x, c, s):
    lane = _iota(x.shape, 1)
    part = jnp.where((lane & 32) == 0, pltpu.roll(x, 96, 1), pltpu.roll(x, 32, 1))
    return x * c + part * s


def _inproj_kernel(h_ref, w_ref, rt_ref, p_ref, qkv_ref, k4_ref, v4_ref):
    j = pl.program_id(1)
    acc = jnp.dot(h_ref[...], w_ref[...], preferred_element_type=F32)
    nb = PROJ_TN // LANES
    tm = acc.shape[0]

    def head_major(dst_ref, x, hb0):
        for hb in range(nb):
            dst_ref[pl.ds(hb0 + hb, tm, stride=ATT_HEADS), :] = x[:, hb * 128:(hb + 1) * 128]

    @pl.when(j < 4)
    def _():
        c, s = rt_ref[:, 0:128], rt_ref[:, 128:256]
        r = jnp.concatenate([_rope128(acc[:, hb * 128:(hb + 1) * 128], c, s) for hb in range(nb)], axis=1)

        @pl.when(j < 2)
        def _():
            qkv_ref[...] = (r * Q_SCALE).astype(BF16)

        for half in range(2):
            @pl.when(j == 2 + half)
            def _():
                qkv_ref[...] = r.astype(BF16)
                head_major(k4_ref, r, half * nb)

    @pl.when((j >= 4) & (j < NJ_QKV))
    def _():
        qkv_ref[...] = acc.astype(BF16)
        for half in range(2):
            @pl.when(j == 4 + half)
            def _():
                head_major(v4_ref, acc, half * nb)

    @pl.when((j >= 6) & (j < 8))
    def _():
        c, s = rt_ref[:, 256:384], rt_ref[:, 384:512]
        for hb in range(nb):
            p_ref[:, hb * 128:(hb + 1) * 128] = _rope64(acc[:, hb * 128:(hb + 1) * 128], c, s)

    @pl.when(j == 12)
    def _():
        c, s = rt_ref[:, 256:384], rt_ref[:, 384:512]
        p_ref[:, 0:128] = _rope64(acc[:, 0:128], c, s)
        p_ref[:, 128:] = acc[:, 128:]

    @pl.when((j >= 8) & (j < 12))
    def _():
        p_ref[...] = acc


def _inproj(h, w, rt):
    m, d = h.shape
    tm = min(m, 512)
    nj = w.shape[1] // PROJ_TN
    return pl.pallas_call(
        _inproj_kernel,
        grid=(m // tm, nj),
        in_specs=[pl.BlockSpec((tm, d), lambda i, j: (i, 0)),
                  pl.BlockSpec((d, PROJ_TN), lambda i, j: (0, j)),
                  pl.BlockSpec((tm, 512), lambda i, j: (i, 0))],
        out_specs=[pl.BlockSpec((tm, PROJ_TN), lambda i, j: (i, jnp.maximum(j - NJ_QKV, 0))),
                   pl.BlockSpec((tm, PROJ_TN), lambda i, j: (i, jnp.minimum(j, NJ_QKV - 1))),
                   pl.BlockSpec((tm * ATT_HEADS, HEAD_DIM), lambda i, j: (i, 0)),
                   pl.BlockSpec((tm * ATT_HEADS, HEAD_DIM), lambda i, j: (i, 0))],
        out_shape=[jax.ShapeDtypeStruct((m, N_P), F32),
                   jax.ShapeDtypeStruct((m, N_QKV), BF16),
                   jax.ShapeDtypeStruct((m * ATT_HEADS, HEAD_DIM), F32),
                   jax.ShapeDtypeStruct((m * ATT_HEADS, HEAD_DIM), F32)],
        compiler_params=_cparams(("parallel", "arbitrary")),
        name="inproj",
    )(h, w, rt)


def _prep_w_in(w):
    parts, start = [], 0
    for sz in IN_SIZES:
        parts.append(w[:, start:start + sz])
        start += sz
    q, k, v, iq, ik, iw, z, xbc, dt, u = parts
    d = w.shape[0]
    small = jnp.concatenate([iw, dt, jnp.zeros((d, LANES - IDX_HEADS - M_HEADS), w.dtype)], axis=1)
    tail = jnp.zeros((d, N_P - C_SM - LANES), w.dtype)
    return jnp.concatenate([q, k, v, iq, xbc, z, u, ik, ik, small, tail], axis=1).astype(BF16)


def _sort_key(x):
    bits = pltpu.bitcast(x, I32)
    return jnp.where(bits < 0, bits ^ jnp.int32(0x7FFFFFFF), bits)


def _radix_select(count_fn, k_eff, shape, idx_bits, n_total):
    kf = k_eff.astype(F32)

    def bit_body(it, carry):
        t, cnt_t = carry
        cand = t + (jnp.int32(1) << (31 - it))
        cnt = count_fn(lambda key, idx: jnp.where(key >= cand, 1.0, 0.0))
        take = cnt >= kf
        return jnp.where(take, cand, t), jnp.where(take, cnt, cnt_t)

    t, cnt_t = lax.fori_loop(0, 32, bit_body,
                             (jnp.full(shape, INT_MIN, I32), jnp.zeros(shape, F32) + n_total))

    def with_ties():
        need = kf - count_fn(lambda key, idx: jnp.where(key > t, 1.0, 0.0))

        def tie_body(it, c):
            cand = c + (jnp.int32(1) << (idx_bits - 1 - it))
            cnt = count_fn(lambda key, idx: jnp.where(key == t, jnp.where(idx < cand, 1.0, 0.0), 0.0))
            return jnp.where(cnt < need, cand, c)

        return lax.fori_loop(0, idx_bits, tie_body, jnp.zeros(shape, I32))

    c = lax.cond(jnp.max(cnt_t - kf) > 0.0, with_ties, lambda: jnp.full(shape, (1 << idx_bits) - 1, I32))
    return t, c


def _selected(key, idx, t, c):
    return jnp.where(key > t, 1.0, jnp.where(key == t, jnp.where(idx <= c, 1.0, 0.0), 0.0))


def _pidx_kernel(iq_ref, ik_ref, wt_ref, bias_ref, key_sc, iqb_sc, *, tq, tk, topk, seq):
    i = pl.program_id(1)
    n_chunks = ((i + 1) * tq + tk - 1) // tk
    iqb_sc[...] = iq_ref[...].astype(BF16)
    wt = wt_ref[...] * ((IDX_HEADS ** -0.5) * (IDX_DIM ** -0.5))
    qpos = i * tq + _iota((1, tq), 1)
    lane_k = _iota((tk, LANES), 1)
    bias_ref[...] = jnp.full(bias_ref.shape, NEG_BIG, bias_ref.dtype)

    def score_chunk(c, carry):
        off = pl.multiple_of(c * tk, tk)
        ik = ik_ref[pl.ds(off, tk), :]
        ik_e = jnp.where(lane_k < IDX_DIM, ik, 0.0).astype(BF16)
        ik_o = jnp.where(lane_k >= IDX_DIM, ik, 0.0).astype(BF16)
        acc = jnp.zeros((tk, tq), F32)
        for p in range(IDX_HEADS // 2):
            qp = iqb_sc[:, p * 128:(p + 1) * 128]
            acc = acc + wt[2 * p:2 * p + 1, :] * jnp.maximum(_nt_dot(ik_e, qp), 0.0)
            acc = acc + wt[2 * p + 1:2 * p + 2, :] * jnp.maximum(_nt_dot(ik_o, qp), 0.0)
        kpos = off + _iota((tk, tq), 0)
        acc = jnp.where(kpos <= qpos, acc, -jnp.inf)
        key_sc[pl.ds(off, tk), :] = _sort_key(acc)
        return carry

    lax.fori_loop(0, n_chunks, score_chunk, 0)
    acc_rows = COUNT_ACC * SUBLANES

    def count_fn(pred):
        def body(c, part):
            off = pl.multiple_of(c * tk, tk)
            key = key_sc[pl.ds(off, tk), :]
            idx = off + _iota((tk, tq), 0)
            m = pred(key, idx)
            return part + jnp.sum(m.reshape(tk // acc_rows, acc_rows, tq), axis=0)
        part = lax.fori_loop(0, n_chunks, body, jnp.zeros((acc_rows, tq), F32))
        return jnp.sum(part, axis=0, keepdims=True)

    k_eff = jnp.minimum(topk, qpos + 1)
    t, c = _radix_select(count_fn, k_eff, (1, tq), max(1, (seq - 1).bit_length()),
                         (n_chunks * tk).astype(F32))

    def write_chunk(cc, carry):
        off = pl.multiple_of(cc * tk, tk)
        key = key_sc[pl.ds(off, tk), :]
        idx = off + _iota((tk, tq), 0)
        bias_t = (_selected(key, idx, t, c) - 1.0) * (-NEG_BIG)
        bias_ref[:, pl.ds(off, tk)] = bias_t.T.astype(bias_ref.dtype)
        return carry

    lax.fori_loop(0, n_chunks, write_chunk, 0)


def _prompt_index_bias(p3, topk):
    b, s, _ = p3.shape
    tq = min(s, 256)
    tk = min(s, 512)
    wt = p3[:, :, C_SM + SM_IW:C_SM + SM_IW + IDX_HEADS].transpose(0, 2, 1)
    kern = functools.partial(_pidx_kernel, tq=tq, tk=tk, topk=topk, seq=s)
    return pl.pallas_call(
        kern,
        grid=(b, s // tq),
        in_specs=[pl.BlockSpec((None, tq, 1024), lambda bb, i: (bb, i, C_IQ // 1024)),
                  pl.BlockSpec((None, s, LANES), lambda bb, i: (bb, 0, C_IK // LANES)),
                  pl.BlockSpec((None, IDX_HEADS, tq), lambda bb, i: (bb, 0, i))],
        out_specs=pl.BlockSpec((None, tq, s), lambda bb, i: (bb, i, 0)),
        out_shape=jax.ShapeDtypeStruct((b, s, s), BF16),
        scratch_shapes=[pltpu.VMEM((s, tq), I32), pltpu.VMEM((tq, 1024), BF16)],
        compiler_params=_cparams(("parallel", "arbitrary")),
        name="prompt_index_bias",
    )(p3, p3, wt)


def _pattn_kernel(qi_ref, ki_ref, q_ref, k_ref, v_ref, b_ref, o_ref, m_sc, l_sc, acc_sc, *, tq, tk):
    st = pl.program_id(1)
    i = qi_ref[st]
    kc = ki_ref[st]
    last = ((i + 1) * tq - 1) // tk
    kt = min(tk, ATT_KT)
    rep = kt // LANES

    @pl.when(kc == 0)
    def _():
        m_sc[...] = jnp.full(m_sc.shape, NEG_BIG, F32)
        l_sc[...] = jnp.zeros(l_sc.shape, F32)
        acc_sc[...] = jnp.zeros(acc_sc.shape, F32)

    ones = jnp.ones((kt, HEAD_DIM), BF16)
    tr = min(tq, ATT_ROWS)
    for r0 in range(0, tq, tr):
        rs = slice(r0, r0 + tr)
        for k0 in range(0, tk, kt):
            ks = slice(k0, k0 + kt)
            bias = b_ref[rs, ks].astype(F32)
            for h in range(ATT_HEADS):
                sl = slice(h * HEAD_DIM, (h + 1) * HEAD_DIM)
                s = _nt_dot(q_ref[rs, sl], k_ref[ks, sl]) + bias
                m_old = m_sc[rs, sl]
                m_new = jnp.maximum(m_old, jnp.max(s, axis=1, keepdims=True))
                alpha = jnp.exp2(m_old - m_new)
                p = jnp.exp2(s - jnp.concatenate([m_new] * rep, axis=1)).astype(BF16)
                pv = jnp.dot(p, jnp.concatenate([v_ref[ks, sl], ones], axis=1), preferred_element_type=F32)
                acc_sc[rs, sl] = alpha * acc_sc[rs, sl] + pv[:, 0:HEAD_DIM]
                l_sc[rs, sl] = alpha * l_sc[rs, sl] + pv[:, HEAD_DIM:]
                m_sc[rs, sl] = m_new

    @pl.when(kc == last)
    def _():
        o_ref[...] = (acc_sc[...] / l_sc[...]).astype(o_ref.dtype)


def _prompt_attention(qkv3, bias):
    b, s, _ = qkv3.shape
    tq = min(s, 256)
    tk = min(s, 512)
    steps = [(i, kc) for i in range(s // tq) for kc in range(((i + 1) * tq - 1) // tk + 1)]
    qi = jnp.asarray(np.array([a for a, _ in steps], np.int32))
    ki = jnp.asarray(np.array([c for _, c in steps], np.int32))
    kern = functools.partial(_pattn_kernel, tq=tq, tk=tk)
    grid_spec = pltpu.PrefetchScalarGridSpec(
        num_scalar_prefetch=2,
        grid=(b, len(steps)),
        in_specs=[pl.BlockSpec((None, tq, ATT_WIDTH), lambda bb, st, qi, ki: (bb, qi[st], C_Q // ATT_WIDTH)),
                  pl.BlockSpec((None, tk, ATT_WIDTH), lambda bb, st, qi, ki: (bb, ki[st], C_K // ATT_WIDTH)),
                  pl.BlockSpec((None, tk, ATT_WIDTH), lambda bb, st, qi, ki: (bb, ki[st], C_V // ATT_WIDTH)),
                  pl.BlockSpec((None, tq, tk), lambda bb, st, qi, ki: (bb, qi[st], ki[st]))],
        out_specs=pl.BlockSpec((None, tq, ATT_WIDTH), lambda bb, st, qi, ki: (bb, qi[st], 0)),
        scratch_shapes=[pltpu.VMEM((tq, ATT_WIDTH), F32)] * 3,
    )
    return pl.pallas_call(
        kern,
        grid_spec=grid_spec,
        out_shape=jax.ShapeDtypeStruct((b, s, ATT_WIDTH), BF16),
        compiler_params=_cparams(("parallel", "arbitrary")),
        name="prompt_attention",
    )(qi, ki, qkv3, qkv3, qkv3, bias)


def _sidx_kernel(pt_ref, iq_ref, w_ref, iknew_ref, *rest, pg, nsteps, past, n_new, topk):
    pages = rest[:pg]
    mask_ref = rest[pg]
    key_sc = rest[pg + 1]
    st = pl.program_id(1)
    iq = iq_ref[...]
    w = w_ref[...]
    width = past + LANES

    def score(ikp):
        d = jnp.maximum(_nt_dot(iq, ikp.astype(BF16)), 0.0) * w
        acc = d[0:SUBLANES]
        for h in range(1, IDX_HEADS):
            acc = acc + d[h * SUBLANES:(h + 1) * SUBLANES]
        return acc

    for r in range(pg):
        off = pl.multiple_of((st * pg + r) * PAGE_SIZE, PAGE_SIZE)
        key_sc[:, pl.ds(off, PAGE_SIZE)] = _sort_key(score(pages[r][...]))

    @pl.when(st == nsteps - 1)
    def _():
        trow = _iota((SUBLANES, LANES), 0)
        jcol = _iota((SUBLANES, LANES), 1)
        sc = score(iknew_ref[...])
        sc = jnp.where(jcol <= trow, jnp.where(jcol < n_new, sc, -jnp.inf), -jnp.inf)
        key_sc[:, past:width] = _sort_key(sc)
        key = key_sc[...]
        idx = _iota((SUBLANES, width), 1)

        def count_fn(pred):
            return jnp.sum(pred(key, idx), axis=1, keepdims=True)

        qpos = past + _iota((SUBLANES, 1), 0)
        k_eff = jnp.minimum(topk, qpos + 1)
        t, c = _radix_select(count_fn, k_eff, (SUBLANES, 1), max(1, (width - 1).bit_length()), float(width))
        mask_ref[...] = _selected(key, idx, t, c)


def _sample_index_mask(page_table, iq_rows, w_col, ik_new, pool_ik, layer, n_new, topk):
    b, npages = page_table.shape
    pg = 8 if npages % 8 == 0 else npages
    nsteps = npages // pg
    past = npages * PAGE_SIZE
    width = past + LANES

    def page_spec(r):
        return pl.BlockSpec((None, None, PAGE_SIZE, IDX_DIM),
                            lambda bb, st, pt: (layer, pt[bb, st * pg + r], 0, 0))

    kern = functools.partial(_sidx_kernel, pg=pg, nsteps=nsteps, past=past, n_new=n_new, topk=topk)
    grid_spec = pltpu.PrefetchScalarGridSpec(
        num_scalar_prefetch=1,
        grid=(b, nsteps),
        in_specs=[pl.BlockSpec((None, IDX_HEADS * SUBLANES, IDX_DIM), lambda bb, st, pt: (bb, 0, 0)),
                  pl.BlockSpec((None, IDX_HEADS * SUBLANES, 1), lambda bb, st, pt: (bb, 0, 0)),
                  pl.BlockSpec((None, LANES, IDX_DIM), lambda bb, st, pt: (bb, 0, 0))]
                 + [page_spec(r) for r in range(pg)],
        out_specs=pl.BlockSpec((None, SUBLANES, width), lambda bb, st, pt: (bb, 0, 0)),
        scratch_shapes=[pltpu.VMEM((SUBLANES, width), I32)],
    )
    return pl.pallas_call(
        kern,
        grid_spec=grid_spec,
        out_shape=jax.ShapeDtypeStruct((b, SUBLANES, width), F32),
        compiler_params=_cparams(("parallel", "arbitrary")),
        name="sample_index_mask",
    )(page_table, iq_rows, w_col, ik_new, *([pool_ik] * pg))


def _sattn_kernel(pt_ref, qbd_ref, mask_ref, knew_ref, vnew_ref, *rest, pg, nsteps, past):
    kpages = rest[:pg]
    vpages = rest[pg:2 * pg]
    o_ref = rest[2 * pg]
    m_sc, l_sc, acc_sc = rest[2 * pg + 1:]
    st = pl.program_id(1)

    @pl.when(st == 0)
    def _():
        m_sc[...] = jnp.full(m_sc.shape, NEG_BIG, F32)
        l_sc[...] = jnp.zeros(l_sc.shape, F32)
        acc_sc[...] = jnp.zeros(acc_sc.shape, F32)

    qbd = qbd_ref[...]

    def page2d(ref):
        return jnp.concatenate([ref[pl.ds(h, PAGE_SIZE, stride=ATT_HEADS), :] for h in range(ATT_HEADS)],
                               axis=1).astype(BF16)

    def update(k_list, v_list, mk8):
        n = len(k_list)
        mk = jnp.concatenate([mk8] * ATT_HEADS, axis=0) > 0
        s = jnp.concatenate([_nt_dot(qbd, k) for k in k_list], axis=1)
        s = jnp.where(mk, s, NEG_BIG)
        m_old = m_sc[...]
        m_new = jnp.maximum(m_old, jnp.max(s, axis=1, keepdims=True))
        alpha = jnp.exp2(m_old - m_new)
        p = jnp.where(mk, jnp.exp2(s - jnp.concatenate([m_new] * n, axis=1)), 0.0)
        l_sc[...] = alpha * l_sc[...] + jnp.sum(p, axis=1, keepdims=True)
        pb = p.astype(BF16)
        pv = jnp.dot(pb[:, 0:LANES], v_list[0], preferred_element_type=F32)
        for r in range(1, n):
            pv = pv + jnp.dot(pb[:, r * LANES:(r + 1) * LANES], v_list[r], preferred_element_type=F32)
        acc_sc[...] = jnp.concatenate([alpha] * ATT_HEADS, axis=1) * acc_sc[...] + pv
        m_sc[...] = m_new

    off = pl.multiple_of(st * pg * PAGE_SIZE, pg * PAGE_SIZE)
    update([page2d(r) for r in kpages], [page2d(r) for r in vpages], mask_ref[:, pl.ds(off, pg * PAGE_SIZE)])

    @pl.when(st == nsteps - 1)
    def _():
        update([knew_ref[...]], [vnew_ref[...]], mask_ref[:, past:past + LANES])
        for h in range(ATT_HEADS):
            rs = slice(h * SUBLANES, (h + 1) * SUBLANES)
            cs = slice(h * HEAD_DIM, (h + 1) * HEAD_DIM)
            o_ref[:, cs] = (acc_sc[rs, cs] / l_sc[rs, :]).astype(o_ref.dtype)


def _sample_attention(page_table, qbd, mask, k_new, v_new, pool_k, pool_v, layer):
    b, npages = page_table.shape
    pg = 8 if npages % 8 == 0 else npages
    nsteps = npages // pg
    past = npages * PAGE_SIZE
    width = past + LANES
    rows = ATT_HEADS * SUBLANES

    def page_spec(r):
        return pl.BlockSpec((None, None, PAGE_SIZE * ATT_HEADS, HEAD_DIM),
                            lambda bb, st, pt: (layer, pt[bb, st * pg + r], 0, 0))

    kern = functools.partial(_sattn_kernel, pg=pg, nsteps=nsteps, past=past)
    grid_spec = pltpu.PrefetchScalarGridSpec(
        num_scalar_prefetch=1,
        grid=(b, nsteps),
        in_specs=[pl.BlockSpec((None, rows, ATT_WIDTH), lambda bb, st, pt: (bb, 0, 0)),
                  pl.BlockSpec((None, SUBLANES, width), lambda bb, st, pt: (bb, 0, 0)),
                  pl.BlockSpec((None, LANES, ATT_WIDTH), lambda bb, st, pt: (bb, 0, 0)),
                  pl.BlockSpec((None, LANES, ATT_WIDTH), lambda bb, st, pt: (bb, 0, 0))]
                 + [page_spec(r) for r in range(pg)] * 2,
        out_specs=pl.BlockSpec((None, SUBLANES, ATT_WIDTH), lambda bb, st, pt: (bb, 0, 0)),
        scratch_shapes=[pltpu.VMEM((rows, LANES), F32), pltpu.VMEM((rows, LANES), F32),
                        pltpu.VMEM((rows, ATT_WIDTH), F32)],
    )
    return pl.pallas_call(
        kern,
        grid_spec=grid_spec,
        out_shape=jax.ShapeDtypeStruct((b, SUBLANES, ATT_WIDTH), BF16),
        compiler_params=_cparams(("parallel", "arbitrary")),
        name="sample_attention",
    )(page_table, qbd, mask, k_new, v_new, *([pool_k] * pg), *([pool_v] * pg))


def _shift_rows(x, prev, d):
    r = pltpu.roll(x, d, 0)
    rowi = _iota(x.shape, 0)
    np_ = prev.shape[0]
    for q in range(d):
        r = jnp.where(rowi == q, prev[np_ - d + q:np_ - d + q + 1, :], r)
    return r


def _per_head(v):
    lane = _iota((v.shape[0], LANES), 1)
    outs = []
    for j in range(M_HEADS // 2):
        a = v[:, SM_DT + 2 * j:SM_DT + 2 * j + 1]
        b = v[:, SM_DT + 2 * j + 1:SM_DT + 2 * j + 2]
        outs.append(jnp.where(lane < M_HEAD_DIM, a, b))
    return jnp.concatenate(outs, axis=1)


def _mamba_kernel(xbc_ref, halo_ref, z_ref, sm_ref, cprev_ref, h0_ref, cw_ref, cb_ref, hp_ref,
                  dexp_ref, ng_ref, y_ref, hout_ref, h_sc, *, rows, n_valid, nsteps):
    c = pl.program_id(1)
    L = SSD_L
    hi = lax.Precision.HIGHEST

    @pl.when(c == 0)
    def _():
        h_sc[...] = h0_ref[...]

    x = xbc_ref[...]
    prev = jnp.where(c == 0, cprev_ref[...], halo_ref[SUBLANES - (M_CONV - 1):SUBLANES, :])
    conv = cb_ref[...] + x * cw_ref[M_CONV - 1:M_CONV, :]
    for d in range(1, M_CONV):
        conv = conv + _shift_rows(x, prev, d) * cw_ref[M_CONV - 1 - d:M_CONV - d, :]
    xc = conv * _sigmoid(conv)

    smv = sm_ref[...] + hp_ref[0:1, :]
    dt = jnp.maximum(smv, 0.0) + jnp.log1p(jnp.exp(-jnp.abs(smv)))
    rowg = c * rows + _iota((rows, LANES), 0)
    dt = jnp.where(rowg < n_valid, dt, 0.0)
    zz = z_ref[...]
    if rows < L:
        xc = jnp.concatenate([xc, jnp.zeros((L - rows, xc.shape[1]), F32)], axis=0)
        dt = jnp.concatenate([dt, jnp.zeros((L - rows, LANES), F32)], axis=0)
        zz = jnp.concatenate([zz, jnp.zeros((L - rows, zz.shape[1]), F32)], axis=0)

    xs = xc[:, 0:M_WIDTH]
    bm = xc[:, M_WIDTH:M_WIDTH + M_GROUPS * M_STATE].astype(BF16)
    cm = xc[:, M_WIDTH + M_GROUPS * M_STATE:].astype(BF16)

    da = dt * (-jnp.exp(hp_ref[1:2, :]))
    ri = _iota((L, L), 0)
    ci = _iota((L, L), 1)
    causal = ri >= ci
    acum = jnp.dot(jnp.where(causal, 1.0, 0.0), da, precision=hi, preferred_element_type=F32)
    acum_t = acum.T
    alast = acum[L - 1:L, :]

    xdt = xs * _per_head(dt)
    xw = xdt * _per_head(jnp.exp(alast - acum))
    xw_t = jnp.concatenate([xw[:, j * LANES:(j + 1) * LANES].T for j in range(M_WIDTH // LANES)], axis=0)
    h_in = h_sc[...]
    gw = M_WIDTH // M_GROUPS

    y_off, states, cb = [], [], []
    for g in range(M_GROUPS):
        bg = bm[:, g * M_STATE:(g + 1) * M_STATE]
        cg = cm[:, g * M_STATE:(g + 1) * M_STATE]
        states.append(jnp.dot(xw_t[g * gw:(g + 1) * gw, :].astype(BF16), bg, preferred_element_type=F32))
        y_off.append(_nt_dot(cg, h_in[g * gw:(g + 1) * gw, :].astype(BF16)))
        cb.append(_nt_dot(cg, bg))
    y = jnp.concatenate(y_off, axis=1) * _per_head(jnp.exp(acum))

    lane = _iota((L, LANES), 1)
    y_diag = []
    for j in range(M_HEADS // 2):
        xpair = xdt[:, j * LANES:(j + 1) * LANES]
        acc = jnp.zeros((L, LANES), F32)
        for e in range(2):
            h = 2 * j + e
            seg = acum[:, SM_DT + h:SM_DT + h + 1] - acum_t[SM_DT + h:SM_DT + h + 1, :]
            mh = (cb[h // (M_HEADS // M_GROUPS)] * jnp.exp(jnp.where(causal, seg, -jnp.inf))).astype(BF16)
            keep = (lane < M_HEAD_DIM) if e == 0 else (lane >= M_HEAD_DIM)
            acc = acc + jnp.dot(mh, jnp.where(keep, xpair, 0.0).astype(BF16), preferred_element_type=F32)
        y_diag.append(acc)
    y = y + jnp.concatenate(y_diag, axis=1) + xs * dexp_ref[...]

    cdec = jnp.concatenate(
        [jnp.broadcast_to(acum_t[SM_DT + h:SM_DT + h + 1, L - 1:L], (M_HEAD_DIM, M_STATE))
         for h in range(M_HEADS)], axis=0)
    h_new = h_in * jnp.exp(cdec) + jnp.concatenate(states, axis=0)
    h_sc[...] = h_new

    @pl.when(c == nsteps - 1)
    def _():
        hout_ref[...] = h_new

    y = y * (zz * _sigmoid(zz))
    outs = []
    for g in range(M_GROUPS):
        seg = y[:, g * gw:(g + 1) * gw]
        ms = jnp.mean(seg * seg, axis=-1, keepdims=True)
        outs.append(seg * lax.rsqrt(ms + EPS) * ng_ref[:, g * gw:(g + 1) * gw])
    y_ref[...] = jnp.concatenate(outs, axis=1)[0:rows].astype(y_ref.dtype)


def _mamba(p3, conv_prev, h0, prm, n_valid):
    b, tp, _ = p3.shape
    rows = min(tp, SSD_L)
    nsteps = tp // rows
    hb = rows // SUBLANES
    kern = functools.partial(_mamba_kernel, rows=rows, n_valid=n_valid, nsteps=nsteps)
    full2 = lambda shape: pl.BlockSpec(shape, lambda bb, c: (0, 0))
    y, hout = pl.pallas_call(
        kern,
        grid=(b, nsteps),
        in_specs=[pl.BlockSpec((None, rows, M_CONV_CH), lambda bb, c: (bb, c, C_XBC // M_CONV_CH)),
                  pl.BlockSpec((None, SUBLANES, M_CONV_CH),
                               lambda bb, c: (bb, jnp.maximum(c * hb - 1, 0), C_XBC // M_CONV_CH)),
                  pl.BlockSpec((None, rows, M_WIDTH), lambda bb, c: (bb, c, C_Z // M_WIDTH)),
                  pl.BlockSpec((None, rows, LANES), lambda bb, c: (bb, c, C_SM // LANES)),
                  pl.BlockSpec((None, M_CONV - 1, M_CONV_CH), lambda bb, c: (bb, 0, 0)),
                  pl.BlockSpec((None, M_WIDTH, M_STATE), lambda bb, c: (bb, 0, 0)),
                  full2((M_CONV, M_CONV_CH)), full2((1, M_CONV_CH)), full2((2, LANES)),
                  full2((1, M_WIDTH)), full2((1, M_WIDTH))],
        out_specs=[pl.BlockSpec((None, rows, M_WIDTH), lambda bb, c: (bb, c, 0)),
                   pl.BlockSpec((None, M_WIDTH, M_STATE), lambda bb, c: (bb, 0, 0))],
        out_shape=[jax.ShapeDtypeStruct((b, tp, M_WIDTH), BF16),
                   jax.ShapeDtypeStruct((b, M_WIDTH, M_STATE), F32)],
        scratch_shapes=[pltpu.VMEM((M_WIDTH, M_STATE), F32)],
        compiler_params=_cparams(("parallel", "arbitrary")),
        name="mamba",
    )(p3, p3, p3, p3, conv_prev.astype(F32), h0.reshape(b, M_WIDTH, M_STATE).astype(F32),
      prm["m_conv_w"], prm["m_conv_b"], prm["m_hp"], prm["m_dexp"], prm["m_ng"])
    return y, hout.reshape(b, M_HEADS, M_HEAD_DIM, M_STATE)


def _gelu_tanh(x):
    return 0.5 * x * (1.0 + jnp.tanh(math.sqrt(2.0 / math.pi) * (x + 0.044715 * (x * x * x))))


def _s5_kernel(u_ref, h0r_ref, h0i_ref, lam_ref, bre_ref, bim_ref, cre_ref, cim_ref, dg_ref, gw_ref,
               o_ref, sr_ref, si_ref, cr_sc, ci_sc, *, rows, n_valid):
    t = pl.program_id(1)

    @pl.when(t == 0)
    def _():
        cr_sc[...] = h0r_ref[...]
        ci_sc[...] = h0i_ref[...]

    lr, li = lam_ref[0:1, :], lam_ref[1:2, :]
    step = jnp.exp(lam_ref[2:3, :])
    mag = jnp.exp(lr * step)
    ar, ai = mag * jnp.cos(li * step), mag * jnp.sin(li * step)
    den = lr * lr + li * li
    gr = ((ar - 1.0) * lr + ai * li) / den
    gi = (ai * lr - (ar - 1.0) * li) / den

    u = u_ref[...]
    ub = u.astype(BF16)
    bur = jnp.dot(ub, bre_ref[...], preferred_element_type=F32)
    bui = jnp.dot(ub, bim_ref[...], preferred_element_type=F32)
    row = _iota((rows, S5_N), 0)
    c_r, c_i = cr_sc[...], ci_sc[...]
    hr = gr * bur - gi * bui + jnp.where(row == 0, ar * c_r - ai * c_i, 0.0)
    hi = gr * bui + gi * bur + jnp.where(row == 0, ar * c_i + ai * c_r, 0.0)

    pr, pi = ar, ai
    d = 1
    while d < rows:
        sr = jnp.where(row >= d, pltpu.roll(hr, d, 0), 0.0)
        si = jnp.where(row >= d, pltpu.roll(hi, d, 0), 0.0)
        hr, hi = hr + pr * sr - pi * si, hi + pr * si + pi * sr
        pr, pi = pr * pr - pi * pi, 2.0 * pr * pi
        d *= 2

    cr_sc[...] = hr[rows - 1:rows, :]
    ci_sc[...] = hi[rows - 1:rows, :]
    r_last = (n_valid - 1) % rows

    @pl.when(t == (n_valid - 1) // rows)
    def _():
        sr_ref[...] = hr[r_last:r_last + 1, :]
        si_ref[...] = hi[r_last:r_last + 1, :]

    y = (jnp.dot(hr.astype(BF16), cre_ref[...], preferred_element_type=F32)
         - jnp.dot(hi.astype(BF16), cim_ref[...], preferred_element_type=F32)
         + dg_ref[0:1, :] * u)
    g = _gelu_tanh(y)
    gate = jnp.dot(g.astype(BF16), gw_ref[...], preferred_element_type=F32) + dg_ref[1:2, :]
    o_ref[...] = (g * _sigmoid(gate)).astype(o_ref.dtype)


def _s5(p3, h0_re, h0_im, prm, n_valid):
    b, tp, _ = p3.shape
    rows = min(tp, 256)
    nsteps = tp // rows
    kern = functools.partial(_s5_kernel, rows=rows, n_valid=n_valid)
    full2 = lambda shape: pl.BlockSpec(shape, lambda bb, t: (0, 0))
    state_spec = pl.BlockSpec((None, 1, S5_N), lambda bb, t: (bb, 0, 0))
    out, sr, si = pl.pallas_call(
        kern,
        grid=(b, nsteps),
        in_specs=[pl.BlockSpec((None, rows, S5_WIDTH), lambda bb, t: (bb, t, C_U // S5_WIDTH)),
                  state_spec, state_spec, full2((3, S5_N)),
                  full2((S5_WIDTH, S5_N)), full2((S5_WIDTH, S5_N)),
                  full2((S5_N, S5_WIDTH)), full2((S5_N, S5_WIDTH)),
                  full2((2, S5_WIDTH)), full2((S5_WIDTH, S5_WIDTH))],
        out_specs=[pl.BlockSpec((None, rows, S5_WIDTH), lambda bb, t: (bb, t, 0)), state_spec, state_spec],
        out_shape=[jax.ShapeDtypeStruct((b, tp, S5_WIDTH), BF16),
                   jax.ShapeDtypeStruct((b, 1, S5_N), F32),
                   jax.ShapeDtypeStruct((b, 1, S5_N), F32)],
        scratch_shapes=[pltpu.VMEM((1, S5_N), F32), pltpu.VMEM((1, S5_N), F32)],
        compiler_params=_cparams(("parallel", "arbitrary")),
        name="s5",
    )(p3, h0_re.reshape(b, 1, S5_N).astype(F32), h0_im.reshape(b, 1, S5_N).astype(F32),
      prm["s5_lam"], prm["s5_bre"], prm["s5_bim"], prm["s5_cre"], prm["s5_cim"], prm["s5_dg"], prm["s5_gw"])
    return out, sr.reshape(b, S5_GROUPS, S5_STATE), si.reshape(b, S5_GROUPS, S5_STATE)


def _block_diag_in(w):
    g, n, k = w.shape
    eye = jnp.eye(g, dtype=w.dtype)
    return jnp.einsum('gnk,gh->gkhn', w, eye).reshape(g * k, g * n)


def _block_diag_out(w):
    g, k, n = w.shape
    eye = jnp.eye(g, dtype=w.dtype)
    return jnp.einsum('gkn,gh->gnhk', w, eye).reshape(g * n, g * k)


def _outproj_kernel(att_ref, mam_ref, s5_ref, w_ref, res_ref, o_ref):
    acc = jnp.dot(att_ref[...], w_ref[0:ATT_WIDTH, :], preferred_element_type=F32)
    acc = acc + jnp.dot(mam_ref[...], w_ref[ATT_WIDTH:ATT_WIDTH + M_WIDTH, :], preferred_element_type=F32)
    acc = acc + jnp.dot(s5_ref[...], w_ref[ATT_WIDTH + M_WIDTH:, :], preferred_element_type=F32)
    o_ref[...] = res_ref[...] + acc


def _outproj(att, mam, s5o, w, res):
    m = att.shape[0]
    tm = min(m, 1024)
    tn = 512
    return pl.pallas_call(
        _outproj_kernel,
        grid=(m // tm, D_MODEL // tn),
        in_specs=[pl.BlockSpec((tm, ATT_WIDTH), lambda i, j: (i, 0)),
                  pl.BlockSpec((tm, M_WIDTH), lambda i, j: (i, 0)),
                  pl.BlockSpec((tm, S5_WIDTH), lambda i, j: (i, 0)),
                  pl.BlockSpec((D_MODEL, tn), lambda i, j: (0, j)),
                  pl.BlockSpec((tm, tn), lambda i, j: (i, j))],
        out_specs=pl.BlockSpec((tm, tn), lambda i, j: (i, j)),
        out_shape=jax.ShapeDtypeStruct((m, D_MODEL), F32),
        compiler_params=_cparams(("parallel", "arbitrary")),
        name="outproj",
    )(att, mam, s5o, w, res)


def _ffn_conv(x, prev, w_ref, b_ref):
    y = b_ref[...] + x * w_ref[FFN_CONV - 1:FFN_CONV, :]
    for d in range(1, FFN_CONV):
        y = y + _shift_rows(x, prev, d) * w_ref[FFN_CONV - 1 - d:FFN_CONV - d, :]
    return y


def _ffn_fused_kernel(h_ref, wg_ref, wv_ref, pg_ref, pv_ref, cwg_ref, cwv_ref, cbg_ref, cbv_ref,
                      act_ref, fg_ref, fv_ref, cg_sc, cv_sc, *, tiles_per_seq):
    first = (pl.program_id(1) % tiles_per_seq) == 0
    h = h_ref[...]
    tm = h.shape[0]

    def branch(w_ref, prev_ref, carry_sc, cw_ref, cb_ref, f_ref):
        x = jnp.dot(h, w_ref[...], preferred_element_type=F32)
        prev = jnp.where(first, prev_ref[...], carry_sc[...])
        y = _ffn_conv(x, prev, cw_ref, cb_ref)
        tail = x[tm - (FFN_CONV - 1):tm, :]
        carry_sc[...] = tail
        f_ref[...] = tail
        return y

    gate = branch(wg_ref, pg_ref, cg_sc, cwg_ref, cbg_ref, fg_ref)
    val = branch(wv_ref, pv_ref, cv_sc, cwv_ref, cbv_ref, fv_ref)
    act_ref[...] = (gate * _sigmoid(gate) * val).astype(act_ref.dtype)


def _ffn_fused(h2, w_up, prev, conv_w, conv_b, b, tp):
    m, d = h2.shape
    tm = min(tp, FFN_TM)
    tps = tp // tm
    tn = 512
    nj = D_FF // tn
    nc = FFN_CONV - 1
    col = lambda o: (lambda j, i: (0, j + o))
    prv = lambda o: pl.BlockSpec((None, nc, tn), lambda j, i: (i // tps, 0, j + o))
    kern = functools.partial(_ffn_fused_kernel, tiles_per_seq=tps)
    act, fg, fv = pl.pallas_call(
        kern,
        grid=(nj, m // tm),
        in_specs=[pl.BlockSpec((tm, d), lambda j, i: (i, 0)),
                  pl.BlockSpec((d, tn), col(0)), pl.BlockSpec((d, tn), col(nj)),
                  prv(0), prv(nj),
                  pl.BlockSpec((FFN_CONV, tn), col(0)), pl.BlockSpec((FFN_CONV, tn), col(nj)),
                  pl.BlockSpec((1, tn), col(0)), pl.BlockSpec((1, tn), col(nj))],
        out_specs=[pl.BlockSpec((tm, tn), lambda j, i: (i, j)),
                   pl.BlockSpec((None, nc, tn), lambda j, i: (i // tps, 0, j)),
                   pl.BlockSpec((None, nc, tn), lambda j, i: (i // tps, 0, j))],
        out_shape=[jax.ShapeDtypeStruct((m, D_FF), BF16),
                   jax.ShapeDtypeStruct((b, nc, D_FF), F32),
                   jax.ShapeDtypeStruct((b, nc, D_FF), F32)],
        scratch_shapes=[pltpu.VMEM((nc, tn), F32), pltpu.VMEM((nc, tn), F32)],
        compiler_params=_cparams(("parallel", "arbitrary")),
        name="ffn_up_act",
    )(h2, w_up, w_up, prev, prev, conv_w, conv_w, conv_b, conv_b)
    return act, jnp.concatenate([fg, fv], axis=-1)


def _matmul_kernel(a_ref, w_ref, o_ref):
    o_ref[...] = jnp.dot(a_ref[...], w_ref[...], preferred_element_type=F32).astype(o_ref.dtype)


def _matmul(a, w, tn=512):
    m, k = a.shape
    n = w.shape[1]
    tm = min(m, 1024)
    return pl.pallas_call(
        _matmul_kernel,
        grid=(m // tm, n // tn),
        in_specs=[pl.BlockSpec((tm, k), lambda i, j: (i, 0)), pl.BlockSpec((k, tn), lambda i, j: (0, j))],
        out_specs=pl.BlockSpec((tm, tn), lambda i, j: (i, j)),
        out_shape=jax.ShapeDtypeStruct((m, n), F32),
        compiler_params=_cparams(("parallel", "arbitrary")),
        name="ffn_up",
    )(a, w)


def _ffn_act_kernel(g_ref, v_ref, gp_ref, vp_ref, gw_ref, vw_ref, gb_ref, vb_ref, o_ref):
    gate = _ffn_conv(g_ref[...], gp_ref[...], gw_ref, gb_ref)
    val = _ffn_conv(v_ref[...], vp_ref[...], vw_ref, vb_ref)
    o_ref[...] = (gate * _sigmoid(gate) * val).astype(o_ref.dtype)


def _ffn_act_short(up3, prev, conv_w, conv_b):
    b, tp, _ = up3.shape
    tc = 512
    nj = D_FF // tc
    nc = FFN_CONV - 1
    main = lambda o: pl.BlockSpec((None, tp, tc), lambda bb, j: (bb, 0, j + o))
    prv = lambda o: pl.BlockSpec((None, nc, tc), lambda bb, j: (bb, 0, j + o))
    wsp = lambda o: pl.BlockSpec((FFN_CONV, tc), lambda bb, j: (0, j + o))
    bsp = lambda o: pl.BlockSpec((1, tc), lambda bb, j: (0, j + o))
    return pl.pallas_call(
        _ffn_act_kernel,
        grid=(b, nj),
        in_specs=[main(0), main(nj), prv(0), prv(nj), wsp(0), wsp(nj), bsp(0), bsp(nj)],
        out_specs=pl.BlockSpec((None, tp, tc), lambda bb, j: (bb, 0, j)),
        out_shape=jax.ShapeDtypeStruct((b, tp, D_FF), BF16),
        compiler_params=_cparams(("parallel", "parallel")),
        name="ffn_act",
    )(up3, up3, prev, prev, conv_w, conv_w, conv_b, conv_b)


def _down_kernel(a_ref, w_ref, res_ref, o_ref):
    o_ref[...] = res_ref[...] + jnp.dot(a_ref[...], w_ref[...], preferred_element_type=F32)


def _ffn_down(act, w, res):
    m, k = act.shape
    tm = min(m, 512)
    tn = 512
    return pl.pallas_call(
        _down_kernel,
        grid=(m // tm, D_MODEL // tn),
        in_specs=[pl.BlockSpec((tm, k), lambda i, j: (i, 0)),
                  pl.BlockSpec((k, tn), lambda i, j: (0, j)),
                  pl.BlockSpec((tm, tn), lambda i, j: (i, j))],
        out_specs=pl.BlockSpec((tm, tn), lambda i, j: (i, j)),
        out_shape=jax.ShapeDtypeStruct((m, D_MODEL), F32),
        compiler_params=_cparams(("parallel", "arbitrary")),
        name="ffn_down",
    )(act, w, res)


def _prep_layer_params(l, w_in, m_conv_w, m_conv_b, m_dt_bias, m_a_log, m_d, m_norm_g,
                       s5_lam_re, s5_lam_im, s5_log_step, s5_b_re, s5_b_im, s5_c_re, s5_c_im,
                       s5_d, s5_glu_w, s5_glu_b, w_out, ffn_w_up, ffn_conv_w, ffn_conv_b, ffn_w_down):
    def lanes(v):
        return jnp.zeros((LANES,), F32).at[SM_DT:SM_DT + M_HEADS].set(v.astype(F32))

    return {
        "w_in": _prep_w_in(w_in[l]),
        "m_conv_w": m_conv_w[l].astype(F32),
        "m_conv_b": m_conv_b[l].reshape(1, M_CONV_CH).astype(F32),
        "m_hp": jnp.stack([lanes(m_dt_bias[l]), lanes(m_a_log[l])], axis=0),
        "m_dexp": jnp.repeat(m_d[l].astype(F32), M_HEAD_DIM).reshape(1, M_WIDTH),
        "m_ng": m_norm_g[l].reshape(1, M_WIDTH).astype(F32),
        "s5_lam": jnp.stack([s5_lam_re[l].reshape(S5_N), s5_lam_im[l].reshape(S5_N),
                             jnp.repeat(s5_log_step[l], S5_STATE)], axis=0).astype(F32),
        "s5_bre": _block_diag_in(s5_b_re[l]).astype(BF16),
        "s5_bim": _block_diag_in(s5_b_im[l]).astype(BF16),
        "s5_cre": _block_diag_out(s5_c_re[l]).astype(BF16),
        "s5_cim": _block_diag_out(s5_c_im[l]).astype(BF16),
        "s5_dg": jnp.stack([s5_d[l], s5_glu_b[l]], axis=0).astype(F32),
        "s5_gw": s5_glu_w[l].astype(BF16),
        "w_out": w_out[l].astype(BF16),
        "ffn_w_up": ffn_w_up[l].astype(BF16),
        "ffn_conv_w": ffn_conv_w[l].astype(F32),
        "ffn_conv_b": ffn_conv_b[l].reshape(1, 2 * D_FF).astype(F32),
        "ffn_w_down": ffn_w_down[l].astype(BF16),
    }


def _layer(x3, rt, attend, mconv_prev, ssm_h0, s5_h0_re, s5_h0_im, fconv_prev, norm1_g, norm2_g, prm, n_valid):
    b, tp, d = x3.shape
    m = b * tp
    x2 = x3.reshape(m, d)
    h = _rmsnorm(x2, norm1_g, BF16)
    p2, qkv2, k4, v4 = _inproj(h, prm["w_in"], rt)
    p3 = p2.reshape(b, tp, N_P)
    att = attend(p3, qkv2.reshape(b, tp, N_QKV))
    mam, ssm_new = _mamba(p3, mconv_prev, ssm_h0, prm, n_valid)
    s5o, s5_re, s5_im = _s5(p3, s5_h0_re, s5_h0_im, prm, n_valid)
    x2 = _outproj(att.reshape(m, ATT_WIDTH), mam.reshape(m, M_WIDTH), s5o.reshape(m, S5_WIDTH), prm["w_out"], x2)
    h2 = _rmsnorm(x2, norm2_g, BF16)
    fconv_prev = fconv_prev.astype(F32)
    if tp >= SSD_L:
        act, fconv_new = _ffn_fused(h2, prm["ffn_w_up"], fconv_prev, prm["ffn_conv_w"], prm["ffn_conv_b"], b, tp)
    else:
        up3 = _matmul(h2, prm["ffn_w_up"]).reshape(b, tp, 2 * D_FF)
        act = _ffn_act_short(up3, fconv_prev, prm["ffn_conv_w"], prm["ffn_conv_b"]).reshape(m, D_FF)
        fconv_new = jnp.concatenate([fconv_prev, up3[:, :n_valid]], axis=1)[:, n_valid:]
    x2 = _ffn_down(act, prm["ffn_w_down"], x2)

    k = k4.reshape(b, tp, ATT_HEADS, HEAD_DIM)[:, :n_valid]
    v = v4.reshape(b, tp, ATT_HEADS, HEAD_DIM)[:, :n_valid]
    ik = p3[:, :n_valid, C_IK:C_IK + IDX_DIM]
    xbc_raw = jnp.concatenate([mconv_prev.astype(F32), p3[:, :n_valid, C_XBC:C_XBC + M_CONV_CH]], axis=1)
    mconv_new = xbc_raw[:, n_valid:]
    return x2.reshape(b, tp, d), (k, v, ik, ssm_new, mconv_new, s5_re, s5_im, fconv_new)


def kernel(x_prompt, x_sample, cache_k, cache_v, cache_idx_k, state_ssm, state_mconv, state_s5_re, state_s5_im,
           state_fconv, page_table, norm1_g, w_in, m_conv_w, m_conv_b, m_dt_bias, m_a_log, m_d, m_norm_g,
           s5_lam_re, s5_lam_im, s5_log_step, s5_b_re, s5_b_im, s5_c_re, s5_c_im, s5_d, s5_glu_w, s5_glu_b,
           w_out, norm2_g, ffn_w_up, ffn_conv_w, ffn_conv_b, ffn_w_down, final_norm_g):
    bp, seq, d = x_prompt.shape
    bs, t_new, _ = x_sample.shape
    depth = w_in.shape[0]
    npages = page_table.shape[1]
    past = npages * PAGE_SIZE
    tpad = SUBLANES
    assert t_new <= tpad and seq % SSD_L == 0

    rt_p = _rope_tables(jnp.arange(seq, dtype=I32), bp)
    rt_s = _rope_tables(past + jnp.arange(tpad, dtype=I32), bs)
    hp = x_prompt.astype(F32)
    hs = jnp.pad(x_sample.astype(F32), ((0, 0), (0, tpad - t_new), (0, 0)))
    n_pool = cache_k.shape[1]
    pool_k = cache_k.reshape(depth, n_pool, PAGE_SIZE * ATT_HEADS, HEAD_DIM)
    pool_v = cache_v.reshape(depth, n_pool, PAGE_SIZE * ATT_HEADS, HEAD_DIM)
    topk_p = min(TOPK_MAX, seq // 4)
    topk_s = min(TOPK_MAX, (past + t_new) // 4)

    zero_mconv = jnp.zeros((bp, M_CONV - 1, M_CONV_CH), F32)
    zero_ssm = jnp.zeros((bp, M_HEADS, M_HEAD_DIM, M_STATE), F32)
    zero_s5 = jnp.zeros((bp, S5_GROUPS, S5_STATE), F32)
    zero_fconv = jnp.zeros((bp, FFN_CONV - 1, 2 * D_FF), F32)

    def attend_prompt(p3, qkv3):
        return _prompt_attention(qkv3, _prompt_index_bias(p3, topk_p))

    new_p, new_s = [], []
    for l in range(depth):
        prm = _prep_layer_params(l, w_in, m_conv_w, m_conv_b, m_dt_bias, m_a_log, m_d, m_norm_g,
                                 s5_lam_re, s5_lam_im, s5_log_step, s5_b_re, s5_b_im, s5_c_re, s5_c_im,
                                 s5_d, s5_glu_w, s5_glu_b, w_out, ffn_w_up, ffn_conv_w, ffn_conv_b, ffn_w_down)

        def attend_sample(p3, qkv3, l=l):
            iq = p3[:, :, C_IQ:C_IQ + IDX_HEADS * IDX_DIM].reshape(bs, tpad, IDX_HEADS, IDX_DIM)
            iq_rows = iq.transpose(0, 2, 1, 3).reshape(bs, IDX_HEADS * tpad, IDX_DIM).astype(BF16)
            iw = p3[:, :, C_SM + SM_IW:C_SM + SM_IW + IDX_HEADS] * ((IDX_HEADS ** -0.5) * (IDX_DIM ** -0.5))
            w_col = iw.transpose(0, 2, 1).reshape(bs, IDX_HEADS * tpad, 1)
            pad_rows = ((0, 0), (0, LANES - tpad), (0, 0))
            ik_new = jnp.pad(p3[:, :, C_IK:C_IK + IDX_DIM], pad_rows)
            mask = _sample_index_mask(page_table, iq_rows, w_col, ik_new, cache_idx_k, l, t_new, topk_s)
            q = qkv3[:, :, C_Q:C_Q + ATT_WIDTH].reshape(bs, tpad, ATT_HEADS, HEAD_DIM).transpose(0, 2, 1, 3)
            eye = jnp.eye(ATT_HEADS, dtype=BF16)
            qbd = (q[:, :, :, None, :] * eye[None, :, None, :, None]).reshape(bs, ATT_HEADS * tpad, ATT_WIDTH)
            k_new = jnp.pad(qkv3[:, :, C_K:C_K + ATT_WIDTH], pad_rows)
            v_new = jnp.pad(qkv3[:, :, C_V:C_V + ATT_WIDTH], pad_rows)
            return _sample_attention(page_table, qbd, mask, k_new, v_new, pool_k, pool_v, l)

        hp, st_p = _layer(hp, rt_p, attend_prompt, zero_mconv, zero_ssm, zero_s5, zero_s5, zero_fconv,
                          norm1_g[l], norm2_g[l], prm, seq)
        hs, st_s = _layer(hs, rt_s, attend_sample, state_mconv[l], state_ssm[l], state_s5_re[l], state_s5_im[l],
                          state_fconv[l], norm1_g[l], norm2_g[l], prm, t_new)
        new_p.append(st_p)
        new_s.append(st_s)

    def stacked(states, i):
        return jnp.stack([s[i] for s in states], axis=0)

    y_prompt = _rmsnorm(hp.reshape(bp * seq, d), final_norm_g, F32).reshape(bp, seq, d)
    y_sample = _rmsnorm(hs.reshape(bs * tpad, d), final_norm_g, F32).reshape(bs, tpad, d)[:, :t_new]
    return (y_prompt, y_sample,
            stacked(new_p, 0), stacked(new_p, 1), stacked(new_p, 2), stacked(new_p, 3),
            stacked(new_p, 4), stacked(new_p, 5), stacked(new_p, 6), stacked(new_p, 7),
            stacked(new_s, 0), stacked(new_s, 1), stacked(new_s, 2), stacked(new_s, 3),
            stacked(new_s, 4), stacked(new_s, 5), stacked(new_s, 6), stacked(new_s, 7))
```

```python
import functools
import math

import numpy as np
import jax
import jax.numpy as jnp
from jax import lax
from jax.experimental import pallas as pl
from jax.experimental.pallas import tpu as pltpu

F32 = jnp.float32
BF16 = jnp.bfloat16
I32 = jnp.int32

D_MODEL = 2048
HEAD_DIM = 128
ATT_WIDTH = D_MODEL // 2
ATT_HEADS = ATT_WIDTH // HEAD_DIM
IDX_HEADS = 16
IDX_DIM = 64
TOPK_MAX = 256
ROPE_THETA = 10000.0
PAGE_SIZE = 128
M_WIDTH = D_MODEL // 4
M_HEAD_DIM = 64
M_HEADS = M_WIDTH // M_HEAD_DIM
M_GROUPS = 2
M_STATE = 128
M_CONV = 4
M_CONV_CH = M_WIDTH + 2 * M_GROUPS * M_STATE
S5_WIDTH = D_MODEL - ATT_WIDTH - M_WIDTH
S5_GROUP = 16
S5_GROUPS = S5_WIDTH // S5_GROUP
S5_STATE = 64
S5_N = S5_GROUPS * S5_STATE
D_FF = (D_MODEL * 11) // 4
FFN_CONV = 3
EPS = 1e-6
IN_SIZES = (ATT_WIDTH, ATT_WIDTH, ATT_WIDTH, IDX_HEADS * IDX_DIM, IDX_DIM, IDX_HEADS,
            M_WIDTH, M_CONV_CH, M_HEADS, S5_WIDTH)

LANES = 128
SUBLANES = 8
VMEM_LIMIT = 48 * 1024 * 1024
INPROJ_VMEM_LIMIT = 56 * 1024 * 1024

N_QKV = 3 * ATT_WIDTH
C_Q, C_K, C_V = 0, 1024, 2048
C_IQ, C_XBC, C_Z, C_U, C_IK, C_SM = 0, 1024, 2048, 2560, 3072, 3200
N_P = 3584
PROJ_TN = 512
NJ_QKV = N_QKV // PROJ_TN
SM_IW = 0
SM_DT = 16
SSD_L = 128
FFN_TM = 1024
COUNT_ACC = 4
ATT_ROWS = 128
S5_KB = 256
NEG_BIG = -1e30
INT_MIN = -2 ** 31
LOG2E = math.log2(math.e)
Q_SCALE = (HEAD_DIM ** -0.5) * LOG2E
ATT_KT = 256


def _cparams(sem):
    return pltpu.CompilerParams(dimension_semantics=sem, vmem_limit_bytes=VMEM_LIMIT)


def _iota(shape, dim):
    return lax.broadcasted_iota(I32, shape, dim)


def _sigmoid(x):
    return 1.0 / (1.0 + jnp.exp(-x))


def _nt_dot(a, b):
    return lax.dot_general(a, b, (((1,), (1,)), ((), ())), preferred_element_type=F32)


def _rmsnorm_kernel(x_ref, g_ref, o_ref):
    x = x_ref[...]
    ms = jnp.mean(x * x, axis=-1, keepdims=True)
    o_ref[...] = (x * lax.rsqrt(ms + EPS) * g_ref[...]).astype(o_ref.dtype)


def _rmsnorm(x2d, g, out_dtype):
    m, d = x2d.shape
    tm = min(m, 512)
    return pl.pallas_call(
        _rmsnorm_kernel,
        grid=(m // tm,),
        in_specs=[pl.BlockSpec((tm, d), lambda i: (i, 0)), pl.BlockSpec((1, d), lambda i: (0, 0))],
        out_specs=pl.BlockSpec((tm, d), lambda i: (i, 0)),
        out_shape=jax.ShapeDtypeStruct((m, d), out_dtype),
        compiler_params=_cparams(("parallel",)),
        name="rmsnorm",
    )(x2d, g.reshape(1, d).astype(F32))


def _rope_tables(pos, batch):
    posf = pos.astype(F32)[:, None]
    half = HEAD_DIM // 2
    inv = ROPE_THETA ** (-jnp.arange(half, dtype=F32) / half)
    ang = posf * inv[None, :]
    c1, s1 = jnp.cos(ang), jnp.sin(ang)
    half2 = IDX_DIM // 2
    inv2 = ROPE_THETA ** (-jnp.arange(half2, dtype=F32) / half2)
    ang2 = posf * inv2[None, :]
    c2, s2 = jnp.cos(ang2), jnp.sin(ang2)
    tab = jnp.concatenate([c1, c1, -s1, s1, c2, c2, c2, c2, -s2, s2, -s2, s2], axis=1)
    return jnp.tile(tab, (batch, 1))


def _rope128(x, c, s):
    return x * c + pltpu.roll(x, 64, 1) * s


def _rope64(x, c, s):
    lane = _iota(x.shape, 1)
    part = jnp.where((lane & 32) == 0, pltpu.roll(x, 96, 1), pltpu.roll(x, 32, 1))
    return x * c + part * s


def _inproj_kernel(h_ref, w_ref, rt_ref, *rest):
    p_ref, qkv_ref, k4_ref, v4_ref = rest[-4:]
    j = pl.program_id(1)
    acc = jnp.dot(h_ref[...], w_ref[...], preferred_element_type=F32)
    nb = PROJ_TN // LANES
    tm = acc.shape[0]

    def head_major(dst_ref, x, hb0):
        for hb in range(nb):
            dst_ref[pl.ds(hb0 + hb, tm, stride=ATT_HEADS), :] = x[:, hb * 128:(hb + 1) * 128]

    @pl.when(j < 4)
    def _():
        c, s = rt_ref[:, 0:128], rt_ref[:, 128:256]
        r = jnp.concatenate([_rope128(acc[:, hb * 128:(hb + 1) * 128], c, s) for hb in range(nb)], axis=1)

        @pl.when(j < 2)
        def _():
            qkv_ref[...] = (r * Q_SCALE).astype(BF16)

        for half in range(2):
            @pl.when(j == 2 + half)
            def _():
                qkv_ref[...] = r.astype(BF16)
                head_major(k4_ref, r, half * nb)

    @pl.when((j >= 4) & (j < NJ_QKV))
    def _():
        qkv_ref[...] = acc.astype(BF16)
        for half in range(2):
            @pl.when(j == 4 + half)
            def _():
                head_major(v4_ref, acc, half * nb)

    @pl.when((j >= 6) & (j < 8))
    def _():
        c, s = rt_ref[:, 256:384], rt_ref[:, 384:512]
        for hb in range(nb):
            p_ref[:, hb * 128:(hb + 1) * 128] = _rope64(acc[:, hb * 128:(hb + 1) * 128], c, s)

    @pl.when(j == 12)
    def _():
        c, s = rt_ref[:, 256:384], rt_ref[:, 384:512]
        p_ref[:, 0:128] = _rope64(acc[:, 0:128], c, s)
        p_ref[:, 128:] = acc[:, 128:]

    @pl.when((j >= 8) & (j < 12))
    def _():
        p_ref[...] = acc


def _inproj(h, w, rt, layer, depth, kv_prev):
    m, d = h.shape
    tm = min(m, 1024)
    nj = w.shape[1] // PROJ_TN
    kv_spec = pl.BlockSpec((None, tm * ATT_HEADS, HEAD_DIM), lambda i, j: (layer, i, 0))
    kv_shape = jax.ShapeDtypeStruct((depth, m * ATT_HEADS, HEAD_DIM), F32)
    in_specs = [pl.BlockSpec((tm, d), lambda i, j: (i, 0)),
                pl.BlockSpec((d, PROJ_TN), lambda i, j: (0, j)),
                pl.BlockSpec((tm, 512), lambda i, j: (i, 0))]
    args = [h, w, rt]
    aliases = {}
    if kv_prev is not None:
        in_specs += [pl.BlockSpec(memory_space=pl.ANY)] * 2
        args += list(kv_prev)
        aliases = {3: 2, 4: 3}
    return pl.pallas_call(
        _inproj_kernel,
        grid=(m // tm, nj),
        in_specs=in_specs,
        out_specs=[pl.BlockSpec((tm, PROJ_TN), lambda i, j: (i, jnp.maximum(j - NJ_QKV, 0))),
                   pl.BlockSpec((tm, PROJ_TN), lambda i, j: (i, jnp.minimum(j, NJ_QKV - 1))),
                   kv_spec, kv_spec],
        out_shape=[jax.ShapeDtypeStruct((m, N_P), F32), jax.ShapeDtypeStruct((m, N_QKV), BF16),
                   kv_shape, kv_shape],
        input_output_aliases=aliases,
        compiler_params=pltpu.CompilerParams(dimension_semantics=("parallel", "arbitrary"),
                                             vmem_limit_bytes=INPROJ_VMEM_LIMIT),
        name="inproj",
    )(*args)


def _prep_w_in(w):
    parts, start = [], 0
    for sz in IN_SIZES:
        parts.append(w[:, start:start + sz])
        start += sz
    q, k, v, iq, ik, iw, z, xbc, dt, u = parts
    d = w.shape[0]
    small = jnp.concatenate([iw, dt, jnp.zeros((d, LANES - IDX_HEADS - M_HEADS), w.dtype)], axis=1)
    tail = jnp.zeros((d, N_P - C_SM - LANES), w.dtype)
    return jnp.concatenate([q, k, v, iq, xbc, z, u, ik, ik, small, tail], axis=1).astype(BF16)


def _sort_key(x):
    bits = pltpu.bitcast(x, I32)
    return jnp.where(bits < 0, bits ^ jnp.int32(0x7FFFFFFF), bits)


def _radix_select(count_fn, k_eff, shape, idx_bits, n_total):
    kf = k_eff.astype(F32)

    def bit_body(it, carry):
        t, cnt_t = carry
        cand = t + (jnp.int32(1) << (31 - it))
        cnt = count_fn(lambda key, idx: jnp.where(key >= cand, 1.0, 0.0))
        take = cnt >= kf
        return jnp.where(take, cand, t), jnp.where(take, cnt, cnt_t)

    t, cnt_t = lax.fori_loop(0, 32, bit_body,
                             (jnp.full(shape, INT_MIN, I32), jnp.zeros(shape, F32) + n_total))

    def with_ties():
        need = kf - count_fn(lambda key, idx: jnp.where(key > t, 1.0, 0.0))

        def tie_body(it, c):
            cand = c + (jnp.int32(1) << (idx_bits - 1 - it))
            cnt = count_fn(lambda key, idx: jnp.where(key == t, jnp.where(idx < cand, 1.0, 0.0), 0.0))
            return jnp.where(cnt < need, cand, c)

        return lax.fori_loop(0, idx_bits, tie_body, jnp.zeros(shape, I32))

    c = lax.cond(jnp.max(cnt_t - kf) > 0.0, with_ties, lambda: jnp.full(shape, (1 << idx_bits) - 1, I32))
    return t, c


def _selected(key, idx, t, c):
    return jnp.where(key > t, 1.0, jnp.where(key == t, jnp.where(idx <= c, 1.0, 0.0), 0.0))


def _pidx_kernel(iq_ref, ik_ref, wt_ref, bias_ref, key_sc, iqb_sc, *, tq, tk, topk, seq):
    i = pl.program_id(1)
    n_chunks = ((i + 1) * tq + tk - 1) // tk
    iqb_sc[...] = iq_ref[...].astype(BF16)
    wt = wt_ref[...] * ((IDX_HEADS ** -0.5) * (IDX_DIM ** -0.5))
    qpos = i * tq + _iota((1, tq), 1)
    lane_k = _iota((tk, LANES), 1)
    bias_ref[...] = jnp.full(bias_ref.shape, NEG_BIG, bias_ref.dtype)

    def score_chunk(c, carry):
        off = pl.multiple_of(c * tk, tk)
        ik = ik_ref[pl.ds(off, tk), :]
        ik_e = jnp.where(lane_k < IDX_DIM, ik, 0.0).astype(BF16)
        ik_o = jnp.where(lane_k >= IDX_DIM, ik, 0.0).astype(BF16)
        acc = jnp.zeros((tk, tq), F32)
        for p in range(IDX_HEADS // 2):
            qp = iqb_sc[:, p * 128:(p + 1) * 128]
            acc = acc + wt[2 * p:2 * p + 1, :] * jnp.maximum(_nt_dot(ik_e, qp), 0.0)
            acc = acc + wt[2 * p + 1:2 * p + 2, :] * jnp.maximum(_nt_dot(ik_o, qp), 0.0)
        kpos = off + _iota((tk, tq), 0)
        acc = jnp.where(kpos <= qpos, acc, -jnp.inf)
        key_sc[pl.ds(off, tk), :] = _sort_key(acc)
        return carry

    lax.fori_loop(0, n_chunks, score_chunk, 0)
    acc_rows = COUNT_ACC * SUBLANES

    def count_fn(pred):
        def body(c, part):
            off = pl.multiple_of(c * tk, tk)
            key = key_sc[pl.ds(off, tk), :]
            idx = off + _iota((tk, tq), 0)
            m = pred(key, idx)
            return part + jnp.sum(m.reshape(tk // acc_rows, acc_rows, tq), axis=0)
        part = lax.fori_loop(0, n_chunks, body, jnp.zeros((acc_rows, tq), F32))
        return jnp.sum(part, axis=0, keepdims=True)

    k_eff = jnp.minimum(topk, qpos + 1)
    t, c = _radix_select(count_fn, k_eff, (1, tq), max(1, (seq - 1).bit_length()),
                         (n_chunks * tk).astype(F32))

    def write_chunk(cc, carry):
        off = pl.multiple_of(cc * tk, tk)
        key = key_sc[pl.ds(off, tk), :]
        idx = off + _iota((tk, tq), 0)
        bias_t = (_selected(key, idx, t, c) - 1.0) * (-NEG_BIG)
        bias_ref[:, pl.ds(off, tk)] = bias_t.T.astype(bias_ref.dtype)
        return carry

    lax.fori_loop(0, n_chunks, write_chunk, 0)


def _prompt_index_bias(p3, topk):
    b, s, _ = p3.shape
    tq = min(s, 256)
    tk = min(s, 512)
    wt = p3[:, :, C_SM + SM_IW:C_SM + SM_IW + IDX_HEADS].transpose(0, 2, 1)
    kern = functools.partial(_pidx_kernel, tq=tq, tk=tk, topk=topk, seq=s)
    return pl.pallas_call(
        kern,
        grid=(b, s // tq),
        in_specs=[pl.BlockSpec((None, tq, 1024), lambda bb, i: (bb, i, C_IQ // 1024)),
                  pl.BlockSpec((None, s, LANES), lambda bb, i: (bb, 0, C_IK // LANES)),
                  pl.BlockSpec((None, IDX_HEADS, tq), lambda bb, i: (bb, 0, i))],
        out_specs=pl.BlockSpec((None, tq, s), lambda bb, i: (bb, i, 0)),
        out_shape=jax.ShapeDtypeStruct((b, s, s), BF16),
        scratch_shapes=[pltpu.VMEM((s, tq), I32), pltpu.VMEM((tq, 1024), BF16)],
        compiler_params=_cparams(("parallel", "arbitrary")),
        name="prompt_index_bias",
    )(p3, p3, wt)


def _pattn_kernel(qi_ref, ki_ref, q_ref, k_ref, v_ref, b_ref, o_ref, m_sc, l_sc, acc_sc, *, tq, tk):
    st = pl.program_id(1)
    i = qi_ref[st]
    kc = ki_ref[st]
    last = ((i + 1) * tq - 1) // tk
    kt = min(tk, ATT_KT)
    rep = kt // LANES

    @pl.when(kc == 0)
    def _():
        m_sc[...] = jnp.full(m_sc.shape, NEG_BIG, F32)
        l_sc[...] = jnp.zeros(l_sc.shape, F32)
        acc_sc[...] = jnp.zeros(acc_sc.shape, F32)

    ones = jnp.ones((kt, HEAD_DIM), BF16)
    tr = min(tq, ATT_ROWS)
    for r0 in range(0, tq, tr):
        rs = slice(r0, r0 + tr)
        for k0 in range(0, tk, kt):
            ks = slice(k0, k0 + kt)
            bias = b_ref[rs, ks].astype(F32)
            for h in range(ATT_HEADS):
                sl = slice(h * HEAD_DIM, (h + 1) * HEAD_DIM)
                s = _nt_dot(q_ref[rs, sl], k_ref[ks, sl]) + bias
                m_old = m_sc[rs, sl]
                m_new = jnp.maximum(m_old, jnp.max(s, axis=1, keepdims=True))
                alpha = jnp.exp2(m_old - m_new)
                p = jnp.exp2(s - jnp.concatenate([m_new] * rep, axis=1)).astype(BF16)
                pv = jnp.dot(p, jnp.concatenate([v_ref[ks, sl], ones], axis=1), preferred_element_type=F32)
                acc_sc[rs, sl] = alpha * acc_sc[rs, sl] + pv[:, 0:HEAD_DIM]
                l_sc[rs, sl] = alpha * l_sc[rs, sl] + pv[:, HEAD_DIM:]
                m_sc[rs, sl] = m_new

    @pl.when(kc == last)
    def _():
        o_ref[...] = (acc_sc[...] / l_sc[...]).astype(o_ref.dtype)


def _prompt_attention(qkv3, bias):
    b, s, _ = qkv3.shape
    tq = min(s, 256)
    tk = min(s, 512)
    steps = [(i, kc) for i in range(s // tq) for kc in range(((i + 1) * tq - 1) // tk + 1)]
    qi = jnp.asarray(np.array([a for a, _ in steps], np.int32))
    ki = jnp.asarray(np.array([c for _, c in steps], np.int32))
    kern = functools.partial(_pattn_kernel, tq=tq, tk=tk)
    grid_spec = pltpu.PrefetchScalarGridSpec(
        num_scalar_prefetch=2,
        grid=(b, len(steps)),
        in_specs=[pl.BlockSpec((None, tq, ATT_WIDTH), lambda bb, st, qi, ki: (bb, qi[st], C_Q // ATT_WIDTH)),
                  pl.BlockSpec((None, tk, ATT_WIDTH), lambda bb, st, qi, ki: (bb, ki[st], C_K // ATT_WIDTH)),
                  pl.BlockSpec((None, tk, ATT_WIDTH), lambda bb, st, qi, ki: (bb, ki[st], C_V // ATT_WIDTH)),
                  pl.BlockSpec((None, tq, tk), lambda bb, st, qi, ki: (bb, qi[st], ki[st]))],
        out_specs=pl.BlockSpec((None, tq, ATT_WIDTH), lambda bb, st, qi, ki: (bb, qi[st], 0)),
        scratch_shapes=[pltpu.VMEM((tq, ATT_WIDTH), F32)] * 3,
    )
    return pl.pallas_call(
        kern,
        grid_spec=grid_spec,
        out_shape=jax.ShapeDtypeStruct((b, s, ATT_WIDTH), BF16),
        compiler_params=_cparams(("parallel", "arbitrary")),
        name="prompt_attention",
    )(qi, ki, qkv3, qkv3, qkv3, bias)


def _sidx_kernel(pt_ref, iq_ref, w_ref, iknew_ref, *rest, pg, nsteps, past, n_new, topk):
    pages = rest[:pg]
    mask_ref = rest[pg]
    key_sc = rest[pg + 1]
    st = pl.program_id(1)
    iq = iq_ref[...]
    w = w_ref[...]
    width = past + LANES

    def score(ikp_t):
        d = jnp.maximum(jnp.dot(iq, ikp_t.astype(BF16), preferred_element_type=F32), 0.0) * w
        acc = d[0:SUBLANES]
        for h in range(1, IDX_HEADS):
            acc = acc + d[h * SUBLANES:(h + 1) * SUBLANES]
        return acc

    for r in range(pg):
        off = pl.multiple_of((st * pg + r) * PAGE_SIZE, PAGE_SIZE)
        key_sc[:, pl.ds(off, PAGE_SIZE)] = _sort_key(score(pages[r][...]))

    @pl.when(st == nsteps - 1)
    def _():
        trow = _iota((SUBLANES, LANES), 0)
        jcol = _iota((SUBLANES, LANES), 1)
        sc = score(iknew_ref[...])
        sc = jnp.where(jcol <= trow, jnp.where(jcol < n_new, sc, -jnp.inf), -jnp.inf)
        key_sc[:, past:width] = _sort_key(sc)
        key = key_sc[...]
        idx = _iota((SUBLANES, width), 1)

        def count_fn(pred):
            return jnp.sum(pred(key, idx), axis=1, keepdims=True)

        qpos = past + _iota((SUBLANES, 1), 0)
        k_eff = jnp.minimum(topk, qpos + 1)
        t, c = _radix_select(count_fn, k_eff, (SUBLANES, 1), max(1, (width - 1).bit_length()), float(width))
        mask_ref[...] = _selected(key, idx, t, c)


def _sample_index_mask(page_table, iq_rows, w_col, ik_new, pool_ik, layer, n_new, topk):
    b, npages = page_table.shape
    pg = 8 if npages % 8 == 0 else npages
    nsteps = npages // pg
    past = npages * PAGE_SIZE
    width = past + LANES

    def page_spec(r):
        return pl.BlockSpec((None, None, IDX_DIM, PAGE_SIZE),
                            lambda bb, st, pt: (layer, pt[bb, st * pg + r], 0, 0))

    kern = functools.partial(_sidx_kernel, pg=pg, nsteps=nsteps, past=past, n_new=n_new, topk=topk)
    grid_spec = pltpu.PrefetchScalarGridSpec(
        num_scalar_prefetch=1,
        grid=(b, nsteps),
        in_specs=[pl.BlockSpec((None, IDX_HEADS * SUBLANES, IDX_DIM), lambda bb, st, pt: (bb, 0, 0)),
                  pl.BlockSpec((None, IDX_HEADS * SUBLANES, 1), lambda bb, st, pt: (bb, 0, 0)),
                  pl.BlockSpec((None, IDX_DIM, LANES), lambda bb, st, pt: (bb, 0, 0))]
                 + [page_spec(r) for r in range(pg)],
        out_specs=pl.BlockSpec((None, SUBLANES, width), lambda bb, st, pt: (bb, 0, 0)),
        scratch_shapes=[pltpu.VMEM((SUBLANES, width), I32)],
    )
    return pl.pallas_call(
        kern,
        grid_spec=grid_spec,
        out_shape=jax.ShapeDtypeStruct((b, SUBLANES, width), F32),
        compiler_params=_cparams(("parallel", "arbitrary")),
        name="sample_index_mask",
    )(page_table, iq_rows, w_col, ik_new, *([pool_ik] * pg))


def _sattn_kernel(pt_ref, qbd_ref, mask_ref, knew_ref, vnew_ref, *rest, pg, nsteps, past):
    kpages = rest[:pg]
    vpages = rest[pg:2 * pg]
    o_ref = rest[2 * pg]
    m_sc, l_sc, acc_sc = rest[2 * pg + 1:]
    st = pl.program_id(1)

    @pl.when(st == 0)
    def _():
        m_sc[...] = jnp.full(m_sc.shape, NEG_BIG, F32)
        l_sc[...] = jnp.zeros(l_sc.shape, F32)
        acc_sc[...] = jnp.zeros(acc_sc.shape, F32)

    qbd = qbd_ref[...]

    def page2d(ref):
        return jnp.concatenate([ref[pl.ds(h, PAGE_SIZE, stride=ATT_HEADS), :] for h in range(ATT_HEADS)],
                               axis=1).astype(BF16)

    def update(k_list, v_list, mk8):
        n = len(k_list)
        mk = jnp.concatenate([mk8] * ATT_HEADS, axis=0) > 0
        s = jnp.concatenate([_nt_dot(qbd, k) for k in k_list], axis=1)
        s = jnp.where(mk, s, NEG_BIG)
        m_old = m_sc[...]
        m_new = jnp.maximum(m_old, jnp.max(s, axis=1, keepdims=True))
        alpha = jnp.exp2(m_old - m_new)
        p = jnp.where(mk, jnp.exp2(s - jnp.concatenate([m_new] * n, axis=1)), 0.0)
        l_sc[...] = alpha * l_sc[...] + jnp.sum(p, axis=1, keepdims=True)
        pb = p.astype(BF16)
        pv = jnp.dot(pb[:, 0:LANES], v_list[0], preferred_element_type=F32)
        for r in range(1, n):
            pv = pv + jnp.dot(pb[:, r * LANES:(r + 1) * LANES], v_list[r], preferred_element_type=F32)
        acc_sc[...] = jnp.concatenate([alpha] * ATT_HEADS, axis=1) * acc_sc[...] + pv
        m_sc[...] = m_new

    off = pl.multiple_of(st * pg * PAGE_SIZE, pg * PAGE_SIZE)
    update([page2d(r) for r in kpages], [page2d(r) for r in vpages], mask_ref[:, pl.ds(off, pg * PAGE_SIZE)])

    @pl.when(st == nsteps - 1)
    def _():
        update([knew_ref[...]], [vnew_ref[...]], mask_ref[:, past:past + LANES])
        for h in range(ATT_HEADS):
            rs = slice(h * SUBLANES, (h + 1) * SUBLANES)
            cs = slice(h * HEAD_DIM, (h + 1) * HEAD_DIM)
            o_ref[:, cs] = (acc_sc[rs, cs] / l_sc[rs, :]).astype(o_ref.dtype)


def _sample_attention(page_table, qbd, mask, k_new, v_new, pool_k, pool_v, layer):
    b, npages = page_table.shape
    pg = 8 if npages % 8 == 0 else npages
    nsteps = npages // pg
    past = npages * PAGE_SIZE
    width = past + LANES
    rows = ATT_HEADS * SUBLANES

    def page_spec(r):
        return pl.BlockSpec((None, None, PAGE_SIZE * ATT_HEADS, HEAD_DIM),
                            lambda bb, st, pt: (layer, pt[bb, st * pg + r], 0, 0))

    kern = functools.partial(_sattn_kernel, pg=pg, nsteps=nsteps, past=past)
    grid_spec = pltpu.PrefetchScalarGridSpec(
        num_scalar_prefetch=1,
        grid=(b, nsteps),
        in_specs=[pl.BlockSpec((None, rows, ATT_WIDTH), lambda bb, st, pt: (bb, 0, 0)),
                  pl.BlockSpec((None, SUBLANES, width), lambda bb, st, pt: (bb, 0, 0)),
                  pl.BlockSpec((None, LANES, ATT_WIDTH), lambda bb, st, pt: (bb, 0, 0)),
                  pl.BlockSpec((None, LANES, ATT_WIDTH), lambda bb, st, pt: (bb, 0, 0))]
                 + [page_spec(r) for r in range(pg)] * 2,
        out_specs=pl.BlockSpec((None, SUBLANES, ATT_WIDTH), lambda bb, st, pt: (bb, 0, 0)),
        scratch_shapes=[pltpu.VMEM((rows, LANES), F32), pltpu.VMEM((rows, LANES), F32),
                        pltpu.VMEM((rows, ATT_WIDTH), F32)],
    )
    return pl.pallas_call(
        kern,
        grid_spec=grid_spec,
        out_shape=jax.ShapeDtypeStruct((b, SUBLANES, ATT_WIDTH), BF16),
        compiler_params=_cparams(("parallel", "arbitrary")),
        name="sample_attention",
    )(page_table, qbd, mask, k_new, v_new, *([pool_k] * pg), *([pool_v] * pg))


def _shift_rows(x, prev, d):
    r = pltpu.roll(x, d, 0)
    rowi = _iota(x.shape, 0)
    np_ = prev.shape[0]
    for q in range(d):
        r = jnp.where(rowi == q, prev[np_ - d + q:np_ - d + q + 1, :], r)
    return r


def _per_head(v):
    lane = _iota((v.shape[0], LANES), 1)
    outs = []
    for j in range(M_HEADS // 2):
        a = v[:, SM_DT + 2 * j:SM_DT + 2 * j + 1]
        b = v[:, SM_DT + 2 * j + 1:SM_DT + 2 * j + 2]
        outs.append(jnp.where(lane < M_HEAD_DIM, a, b))
    return jnp.concatenate(outs, axis=1)


def _mamba_kernel(xbc_ref, halo_ref, z_ref, sm_ref, cprev_ref, h0_ref, cw_ref, cb_ref, hp_ref,
                  dexp_ref, ng_ref, y_ref, hout_ref, h_sc, *, rows, n_valid, nsteps):
    c = pl.program_id(1)
    L = SSD_L
    hi = lax.Precision.HIGHEST

    @pl.when(c == 0)
    def _():
        h_sc[...] = h0_ref[...]

    x = xbc_ref[...]
    prev = jnp.where(c == 0, cprev_ref[...], halo_ref[SUBLANES - (M_CONV - 1):SUBLANES, :])
    conv = cb_ref[...] + x * cw_ref[M_CONV - 1:M_CONV, :]
    for d in range(1, M_CONV):
        conv = conv + _shift_rows(x, prev, d) * cw_ref[M_CONV - 1 - d:M_CONV - d, :]
    xc = conv * _sigmoid(conv)

    smv = sm_ref[...] + hp_ref[0:1, :]
    dt = jnp.maximum(smv, 0.0) + jnp.log1p(jnp.exp(-jnp.abs(smv)))
    rowg = c * rows + _iota((rows, LANES), 0)
    dt = jnp.where(rowg < n_valid, dt, 0.0)
    zz = z_ref[...]
    if rows < L:
        xc = jnp.concatenate([xc, jnp.zeros((L - rows, xc.shape[1]), F32)], axis=0)
        dt = jnp.concatenate([dt, jnp.zeros((L - rows, LANES), F32)], axis=0)
        zz = jnp.concatenate([zz, jnp.zeros((L - rows, zz.shape[1]), F32)], axis=0)

    xs = xc[:, 0:M_WIDTH]
    bm = xc[:, M_WIDTH:M_WIDTH + M_GROUPS * M_STATE].astype(BF16)
    cm = xc[:, M_WIDTH + M_GROUPS * M_STATE:].astype(BF16)

    da = dt * (-jnp.exp(hp_ref[1:2, :]))
    ri = _iota((L, L), 0)
    ci = _iota((L, L), 1)
    causal = ri >= ci
    acum = jnp.dot(jnp.where(causal, 1.0, 0.0), da, precision=hi, preferred_element_type=F32)
    acum_t = acum.T
    alast = acum[L - 1:L, :]

    xdt = xs * _per_head(dt)
    xw = xdt * _per_head(jnp.exp(alast - acum))
    xw_t = jnp.concatenate([xw[:, j * LANES:(j + 1) * LANES].T for j in range(M_WIDTH // LANES)], axis=0)
    h_in = h_sc[...]
    gw = M_WIDTH // M_GROUPS

    y_off, states, cb = [], [], []
    for g in range(M_GROUPS):
        bg = bm[:, g * M_STATE:(g + 1) * M_STATE]
        cg = cm[:, g * M_STATE:(g + 1) * M_STATE]
        states.append(jnp.dot(xw_t[g * gw:(g + 1) * gw, :].astype(BF16), bg, preferred_element_type=F32))
        y_off.append(_nt_dot(cg, h_in[g * gw:(g + 1) * gw, :].astype(BF16)))
        cb.append(_nt_dot(cg, bg))
    y = jnp.concatenate(y_off, axis=1) * _per_head(jnp.exp(acum))

    lane = _iota((L, LANES), 1)
    y_diag = []
    for j in range(M_HEADS // 2):
        xpair = xdt[:, j * LANES:(j + 1) * LANES]
        acc = jnp.zeros((L, LANES), F32)
        for e in range(2):
            h = 2 * j + e
            seg = acum[:, SM_DT + h:SM_DT + h + 1] - acum_t[SM_DT + h:SM_DT + h + 1, :]
            mh = (cb[h // (M_HEADS // M_GROUPS)] * jnp.exp(jnp.where(causal, seg, -jnp.inf))).astype(BF16)
            keep = (lane < M_HEAD_DIM) if e == 0 else (lane >= M_HEAD_DIM)
            acc = acc + jnp.dot(mh, jnp.where(keep, xpair, 0.0).astype(BF16), preferred_element_type=F32)
        y_diag.append(acc)
    y = y + jnp.concatenate(y_diag, axis=1) + xs * dexp_ref[...]

    cdec = jnp.concatenate(
        [jnp.broadcast_to(acum_t[SM_DT + h:SM_DT + h + 1, L - 1:L], (M_HEAD_DIM, M_STATE))
         for h in range(M_HEADS)], axis=0)
    h_new = h_in * jnp.exp(cdec) + jnp.concatenate(states, axis=0)
    h_sc[...] = h_new

    @pl.when(c == nsteps - 1)
    def _():
        hout_ref[...] = h_new

    y = y * (zz * _sigmoid(zz))
    outs = []
    for g in range(M_GROUPS):
        seg = y[:, g * gw:(g + 1) * gw]
        ms = jnp.mean(seg * seg, axis=-1, keepdims=True)
        outs.append(seg * lax.rsqrt(ms + EPS) * ng_ref[:, g * gw:(g + 1) * gw])
    y_ref[...] = jnp.concatenate(outs, axis=1)[0:rows].astype(y_ref.dtype)


def _mamba(p3, conv_prev, h0, prm, n_valid):
    b, tp, _ = p3.shape
    rows = min(tp, SSD_L)
    nsteps = tp // rows
    hb = rows // SUBLANES
    kern = functools.partial(_mamba_kernel, rows=rows, n_valid=n_valid, nsteps=nsteps)
    full2 = lambda shape: pl.BlockSpec(shape, lambda bb, c: (0, 0))
    y, hout = pl.pallas_call(
        kern,
        grid=(b, nsteps),
        in_specs=[pl.BlockSpec((None, rows, M_CONV_CH), lambda bb, c: (bb, c, C_XBC // M_CONV_CH)),
                  pl.BlockSpec((None, SUBLANES, M_CONV_CH),
                               lambda bb, c: (bb, jnp.maximum(c * hb - 1, 0), C_XBC // M_CONV_CH)),
                  pl.BlockSpec((None, rows, M_WIDTH), lambda bb, c: (bb, c, C_Z // M_WIDTH)),
                  pl.BlockSpec((None, rows, LANES), lambda bb, c: (bb, c, C_SM // LANES)),
                  pl.BlockSpec((None, M_CONV - 1, M_CONV_CH), lambda bb, c: (bb, 0, 0)),
                  pl.BlockSpec((None, M_WIDTH, M_STATE), lambda bb, c: (bb, 0, 0)),
                  full2((M_CONV, M_CONV_CH)), full2((1, M_CONV_CH)), full2((2, LANES)),
                  full2((1, M_WIDTH)), full2((1, M_WIDTH))],
        out_specs=[pl.BlockSpec((None, rows, M_WIDTH), lambda bb, c: (bb, c, 0)),
                   pl.BlockSpec((None, M_WIDTH, M_STATE), lambda bb, c: (bb, 0, 0))],
        out_shape=[jax.ShapeDtypeStruct((b, tp, M_WIDTH), BF16),
                   jax.ShapeDtypeStruct((b, M_WIDTH, M_STATE), F32)],
        scratch_shapes=[pltpu.VMEM((M_WIDTH, M_STATE), F32)],
        compiler_params=_cparams(("parallel", "arbitrary")),
        name="mamba",
    )(p3, p3, p3, p3, conv_prev.astype(F32), h0.reshape(b, M_WIDTH, M_STATE).astype(F32),
      prm["m_conv_w"], prm["m_conv_b"], prm["m_hp"], prm["m_dexp"], prm["m_ng"])
    return y, hout.reshape(b, M_HEADS, M_HEAD_DIM, M_STATE)


def _gelu_tanh(x):
    return 0.5 * x * (1.0 + jnp.tanh(math.sqrt(2.0 / math.pi) * (x + 0.044715 * (x * x * x))))


def _s5_kernel(u_ref, h0r_ref, h0i_ref, lam_ref, bre_ref, bim_ref, cre_ref, cim_ref, dg_ref, gw_ref,
               o_ref, sr_ref, si_ref, cr_sc, ci_sc, *, rows, n_valid):
    t = pl.program_id(1)

    @pl.when(t == 0)
    def _():
        cr_sc[...] = h0r_ref[...]
        ci_sc[...] = h0i_ref[...]

    lr, li = lam_ref[0:1, :], lam_ref[1:2, :]
    step = jnp.exp(lam_ref[2:3, :])
    mag = jnp.exp(lr * step)
    ar, ai = mag * jnp.cos(li * step), mag * jnp.sin(li * step)
    den = lr * lr + li * li
    gr = ((ar - 1.0) * lr + ai * li) / den
    gi = (ai * lr - (ar - 1.0) * li) / den

    nslab = S5_WIDTH // S5_KB
    nb = S5_N // nslab

    def in_proj(x, w_ref):
        return jnp.concatenate(
            [jnp.dot(x[:, k * S5_KB:(k + 1) * S5_KB], w_ref[k * S5_KB:(k + 1) * S5_KB, k * nb:(k + 1) * nb],
                     preferred_element_type=F32) for k in range(nslab)], axis=1)

    def out_proj(x, w_ref):
        return jnp.concatenate(
            [jnp.dot(x[:, k * nb:(k + 1) * nb], w_ref[k * nb:(k + 1) * nb, k * S5_KB:(k + 1) * S5_KB],
                     preferred_element_type=F32) for k in range(nslab)], axis=1)

    u = u_ref[...]
    ub = u.astype(BF16)
    bur = in_proj(ub, bre_ref)
    bui = in_proj(ub, bim_ref)

    seg = rows // SUBLANES
    shape8 = (SUBLANES, S5_N)
    sub = _iota(shape8, 0)
    a8r, a8i = jnp.broadcast_to(ar, shape8), jnp.broadcast_to(ai, shape8)
    g8r, g8i = jnp.broadcast_to(gr, shape8), jnp.broadcast_to(gi, shape8)

    hr_t, hi_t = [], []
    for j in range(seg):
        xr = bur[j * SUBLANES:(j + 1) * SUBLANES, :]
        xi = bui[j * SUBLANES:(j + 1) * SUBLANES, :]
        tr = g8r * xr - g8i * xi
        ti = g8r * xi + g8i * xr
        if j > 0:
            tr, ti = tr + a8r * hr_t[-1] - a8i * hi_t[-1], ti + a8r * hi_t[-1] + a8i * hr_t[-1]
        hr_t.append(tr)
        hi_t.append(ti)

    mr, mi = a8r, a8i
    n = 1
    while n < seg:
        mr, mi = mr * mr - mi * mi, 2.0 * mr * mi
        n *= 2
    c0r = jnp.broadcast_to(cr_sc[...], shape8)
    c0i = jnp.broadcast_to(ci_sc[...], shape8)
    er = hr_t[-1] + jnp.where(sub == 0, mr * c0r - mi * c0i, 0.0)
    ei = hi_t[-1] + jnp.where(sub == 0, mr * c0i + mi * c0r, 0.0)
    d = 1
    while d < SUBLANES:
        sr = jnp.where(sub >= d, pltpu.roll(er, d, 0), 0.0)
        si = jnp.where(sub >= d, pltpu.roll(ei, d, 0), 0.0)
        er, ei = er + mr * sr - mi * si, ei + mr * si + mi * sr
        mr, mi = mr * mr - mi * mi, 2.0 * mr * mi
        d *= 2
    wr = jnp.where(sub == 0, c0r, pltpu.roll(er, 1, 0))
    wi = jnp.where(sub == 0, c0i, pltpu.roll(ei, 1, 0))
    for j in range(seg):
        wr, wi = a8r * wr - a8i * wi, a8r * wi + a8i * wr
        hr_t[j] = hr_t[j] + wr
        hi_t[j] = hi_t[j] + wi

    cr_sc[...] = er[SUBLANES - 1:SUBLANES, :]
    ci_sc[...] = ei[SUBLANES - 1:SUBLANES, :]
    r_last = (n_valid - 1) % rows
    s_last, j_last = r_last // seg, r_last % seg

    @pl.when(t == (n_valid - 1) // rows)
    def _():
        sr_ref[...] = hr_t[j_last][s_last:s_last + 1, :]
        si_ref[...] = hi_t[j_last][s_last:s_last + 1, :]

    hr = jnp.concatenate(hr_t, axis=0)
    hi = jnp.concatenate(hi_t, axis=0)
    y = out_proj(hr.astype(BF16), cre_ref) - out_proj(hi.astype(BF16), cim_ref) + dg_ref[0:1, :] * u
    g = _gelu_tanh(y)
    gate = jnp.dot(g.astype(BF16), gw_ref[...], preferred_element_type=F32) + dg_ref[1:2, :]
    o_ref[...] = (g * _sigmoid(gate)).astype(o_ref.dtype)


def _s5(p3, h0_re, h0_im, prm, n_valid):
    b, tp, _ = p3.shape
    rows = min(tp, 256)
    nsteps = tp // rows
    seg = rows // SUBLANES

    def permute(x, a, c):
        return x.reshape(b, nsteps, a, c, S5_WIDTH).swapaxes(2, 3).reshape(b, tp, S5_WIDTH)

    u3 = p3[:, :, C_U:C_U + S5_WIDTH]
    if seg > 1:
        u3 = permute(u3, SUBLANES, seg)
    kern = functools.partial(_s5_kernel, rows=rows, n_valid=n_valid)
    full2 = lambda shape: pl.BlockSpec(shape, lambda bb, t: (0, 0))
    state_spec = pl.BlockSpec((None, 1, S5_N), lambda bb, t: (bb, 0, 0))
    out, sr, si = pl.pallas_call(
        kern,
        grid=(b, nsteps),
        in_specs=[pl.BlockSpec((None, rows, S5_WIDTH), lambda bb, t: (bb, t, 0)),
                  state_spec, state_spec, full2((3, S5_N)),
                  full2((S5_WIDTH, S5_N)), full2((S5_WIDTH, S5_N)),
                  full2((S5_N, S5_WIDTH)), full2((S5_N, S5_WIDTH)),
                  full2((2, S5_WIDTH)), full2((S5_WIDTH, S5_WIDTH))],
        out_specs=[pl.BlockSpec((None, rows, S5_WIDTH), lambda bb, t: (bb, t, 0)), state_spec, state_spec],
        out_shape=[jax.ShapeDtypeStruct((b, tp, S5_WIDTH), BF16),
                   jax.ShapeDtypeStruct((b, 1, S5_N), F32),
                   jax.ShapeDtypeStruct((b, 1, S5_N), F32)],
        scratch_shapes=[pltpu.VMEM((1, S5_N), F32), pltpu.VMEM((1, S5_N), F32)],
        compiler_params=_cparams(("parallel", "arbitrary")),
        name="s5",
    )(u3, h0_re.reshape(b, 1, S5_N).astype(F32), h0_im.reshape(b, 1, S5_N).astype(F32),
      prm["s5_lam"], prm["s5_bre"], prm["s5_bim"], prm["s5_cre"], prm["s5_cim"], prm["s5_dg"], prm["s5_gw"])
    if seg > 1:
        out = permute(out, seg, SUBLANES)
    return out, sr.reshape(b, S5_GROUPS, S5_STATE), si.reshape(b, S5_GROUPS, S5_STATE)


def _block_diag_in(w):
    g, n, k = w.shape
    eye = jnp.eye(g, dtype=w.dtype)
    return jnp.einsum('gnk,gh->gkhn', w, eye).reshape(g * k, g * n)


def _block_diag_out(w):
    g, k, n = w.shape
    eye = jnp.eye(g, dtype=w.dtype)
    return jnp.einsum('gkn,gh->gnhk', w, eye).reshape(g * n, g * k)


def _outproj_kernel(att_ref, mam_ref, s5_ref, w_ref, res_ref, o_ref):
    acc = jnp.dot(att_ref[...], w_ref[0:ATT_WIDTH, :], preferred_element_type=F32)
    acc = acc + jnp.dot(mam_ref[...], w_ref[ATT_WIDTH:ATT_WIDTH + M_WIDTH, :], preferred_element_type=F32)
    acc = acc + jnp.dot(s5_ref[...], w_ref[ATT_WIDTH + M_WIDTH:, :], preferred_element_type=F32)
    o_ref[...] = res_ref[...] + acc


def _outproj(att, mam, s5o, w_all, layer, res):
    m = att.shape[0]
    tm = min(m, 1024)
    tn = 512
    return pl.pallas_call(
        _outproj_kernel,
        grid=(m // tm, D_MODEL // tn),
        in_specs=[pl.BlockSpec((tm, ATT_WIDTH), lambda i, j: (i, 0)),
                  pl.BlockSpec((tm, M_WIDTH), lambda i, j: (i, 0)),
                  pl.BlockSpec((tm, S5_WIDTH), lambda i, j: (i, 0)),
                  pl.BlockSpec((None, D_MODEL, tn), lambda i, j: (layer, 0, j)),
                  pl.BlockSpec((tm, tn), lambda i, j: (i, j))],
        out_specs=pl.BlockSpec((tm, tn), lambda i, j: (i, j)),
        out_shape=jax.ShapeDtypeStruct((m, D_MODEL), F32),
        compiler_params=_cparams(("parallel", "arbitrary")),
        name="outproj",
    )(att, mam, s5o, w_all, res)


def _ffn_conv(x, prev, w_ref, b_ref):
    y = b_ref[...] + x * w_ref[FFN_CONV - 1:FFN_CONV, :]
    for d in range(1, FFN_CONV):
        y = y + _shift_rows(x, prev, d) * w_ref[FFN_CONV - 1 - d:FFN_CONV - d, :]
    return y


def _ffn_fused_kernel(h_ref, wg_ref, wv_ref, pg_ref, pv_ref, cwg_ref, cwv_ref, cbg_ref, cbv_ref,
                      act_ref, fg_ref, fv_ref, cg_sc, cv_sc, *, tiles_per_seq):
    first = (pl.program_id(1) % tiles_per_seq) == 0
    h = h_ref[...]
    tm = h.shape[0]

    def branch(w_ref, prev_ref, carry_sc, cw_ref, cb_ref, f_ref):
        x = jnp.dot(h, w_ref[...], preferred_element_type=F32)
        prev = jnp.where(first, prev_ref[...], carry_sc[...])
        y = _ffn_conv(x, prev, cw_ref, cb_ref)
        tail = x[tm - (FFN_CONV - 1):tm, :]
        carry_sc[...] = tail
        f_ref[...] = tail
        return y

    gate = branch(wg_ref, pg_ref, cg_sc, cwg_ref, cbg_ref, fg_ref)
    val = branch(wv_ref, pv_ref, cv_sc, cwv_ref, cbv_ref, fv_ref)
    act_ref[...] = (gate * _sigmoid(gate) * val).astype(act_ref.dtype)


def _ffn_fused(h2, w_up_all, layer, prev, conv_w, conv_b, b, tp):
    m, d = h2.shape
    tm = min(tp, FFN_TM)
    tps = tp // tm
    tn = 512
    nj = D_FF // tn
    nc = FFN_CONV - 1
    col = lambda o: (lambda j, i: (0, j + o))
    wcol = lambda o: (lambda j, i: (layer, 0, j + o))
    prv = lambda o: pl.BlockSpec((None, nc, tn), lambda j, i: (i // tps, 0, j + o))
    kern = functools.partial(_ffn_fused_kernel, tiles_per_seq=tps)
    act, fg, fv = pl.pallas_call(
        kern,
        grid=(nj, m // tm),
        in_specs=[pl.BlockSpec((tm, d), lambda j, i: (i, 0)),
                  pl.BlockSpec((None, d, tn), wcol(0)), pl.BlockSpec((None, d, tn), wcol(nj)),
                  prv(0), prv(nj),
                  pl.BlockSpec((FFN_CONV, tn), col(0)), pl.BlockSpec((FFN_CONV, tn), col(nj)),
                  pl.BlockSpec((1, tn), col(0)), pl.BlockSpec((1, tn), col(nj))],
        out_specs=[pl.BlockSpec((tm, tn), lambda j, i: (i, j)),
                   pl.BlockSpec((None, nc, tn), lambda j, i: (i // tps, 0, j)),
                   pl.BlockSpec((None, nc, tn), lambda j, i: (i // tps, 0, j))],
        out_shape=[jax.ShapeDtypeStruct((m, D_FF), BF16),
                   jax.ShapeDtypeStruct((b, nc, D_FF), F32),
                   jax.ShapeDtypeStruct((b, nc, D_FF), F32)],
        scratch_shapes=[pltpu.VMEM((nc, tn), F32), pltpu.VMEM((nc, tn), F32)],
        compiler_params=_cparams(("parallel", "arbitrary")),
        name="ffn_up_act",
    )(h2, w_up_all, w_up_all, prev, prev, conv_w, conv_w, conv_b, conv_b)
    return act, jnp.concatenate([fg, fv], axis=-1)


def _matmul_kernel(a_ref, w_ref, o_ref):
    o_ref[...] = jnp.dot(a_ref[...], w_ref[...], preferred_element_type=F32).astype(o_ref.dtype)


def _ffn_up(a, w_all, layer, tn=512):
    m, k = a.shape
    n = w_all.shape[2]
    tm = min(m, 1024)
    return pl.pallas_call(
        _matmul_kernel,
        grid=(m // tm, n // tn),
        in_specs=[pl.BlockSpec((tm, k), lambda i, j: (i, 0)),
                  pl.BlockSpec((None, k, tn), lambda i, j: (layer, 0, j))],
        out_specs=pl.BlockSpec((tm, tn), lambda i, j: (i, j)),
        out_shape=jax.ShapeDtypeStruct((m, n), F32),
        compiler_params=_cparams(("parallel", "arbitrary")),
        name="ffn_up",
    )(a, w_all)


def _ffn_act_kernel(g_ref, v_ref, gp_ref, vp_ref, gw_ref, vw_ref, gb_ref, vb_ref, o_ref):
    gate = _ffn_conv(g_ref[...], gp_ref[...], gw_ref, gb_ref)
    val = _ffn_conv(v_ref[...], vp_ref[...], vw_ref, vb_ref)
    o_ref[...] = (gate * _sigmoid(gate) * val).astype(o_ref.dtype)


def _ffn_act_short(up3, prev, conv_w, conv_b):
    b, tp, _ = up3.shape
    tc = 512
    nj = D_FF // tc
    nc = FFN_CONV - 1
    main = lambda o: pl.BlockSpec((None, tp, tc), lambda bb, j: (bb, 0, j + o))
    prv = lambda o: pl.BlockSpec((None, nc, tc), lambda bb, j: (bb, 0, j + o))
    wsp = lambda o: pl.BlockSpec((FFN_CONV, tc), lambda bb, j: (0, j + o))
    bsp = lambda o: pl.BlockSpec((1, tc), lambda bb, j: (0, j + o))
    return pl.pallas_call(
        _ffn_act_kernel,
        grid=(b, nj),
        in_specs=[main(0), main(nj), prv(0), prv(nj), wsp(0), wsp(nj), bsp(0), bsp(nj)],
        out_specs=pl.BlockSpec((None, tp, tc), lambda bb, j: (bb, 0, j)),
        out_shape=jax.ShapeDtypeStruct((b, tp, D_FF), BF16),
        compiler_params=_cparams(("parallel", "parallel")),
        name="ffn_act",
    )(up3, up3, prev, prev, conv_w, conv_w, conv_b, conv_b)


def _down_kernel(a_ref, w_ref, res_ref, o_ref):
    o_ref[...] = res_ref[...] + jnp.dot(a_ref[...], w_ref[...], preferred_element_type=F32)


def _ffn_down(act, w_all, layer, res):
    m, k = act.shape
    tm = min(m, 1024)
    tn = 256
    return pl.pallas_call(
        _down_kernel,
        grid=(m // tm, D_MODEL // tn),
        in_specs=[pl.BlockSpec((tm, k), lambda i, j: (i, 0)),
                  pl.BlockSpec((None, k, tn), lambda i, j: (layer, 0, j)),
                  pl.BlockSpec((tm, tn), lambda i, j: (i, j))],
        out_specs=pl.BlockSpec((tm, tn), lambda i, j: (i, j)),
        out_shape=jax.ShapeDtypeStruct((m, D_MODEL), F32),
        compiler_params=_cparams(("parallel", "arbitrary")),
        name="ffn_down",
    )(act, w_all, res)


def _prep_layer_params(l, w_in, m_conv_w, m_conv_b, m_dt_bias, m_a_log, m_d, m_norm_g,
                       s5_lam_re, s5_lam_im, s5_log_step, s5_b_re, s5_b_im, s5_c_re, s5_c_im,
                       s5_d, s5_glu_w, s5_glu_b, w_out, ffn_w_up, ffn_conv_w, ffn_conv_b, ffn_w_down):
    def lanes(v):
        return jnp.zeros((LANES,), F32).at[SM_DT:SM_DT + M_HEADS].set(v.astype(F32))

    return {
        "w_in": _prep_w_in(w_in[l]),
        "m_conv_w": m_conv_w[l].astype(F32),
        "m_conv_b": m_conv_b[l].reshape(1, M_CONV_CH).astype(F32),
        "m_hp": jnp.stack([lanes(m_dt_bias[l]), lanes(m_a_log[l])], axis=0),
        "m_dexp": jnp.repeat(m_d[l].astype(F32), M_HEAD_DIM).reshape(1, M_WIDTH),
        "m_ng": m_norm_g[l].reshape(1, M_WIDTH).astype(F32),
        "s5_lam": jnp.stack([s5_lam_re[l].reshape(S5_N), s5_lam_im[l].reshape(S5_N),
                             jnp.repeat(s5_log_step[l], S5_STATE)], axis=0).astype(F32),
        "s5_bre": _block_diag_in(s5_b_re[l]).astype(BF16),
        "s5_bim": _block_diag_in(s5_b_im[l]).astype(BF16),
        "s5_cre": _block_diag_out(s5_c_re[l]).astype(BF16),
        "s5_cim": _block_diag_out(s5_c_im[l]).astype(BF16),
        "s5_dg": jnp.stack([s5_d[l], s5_glu_b[l]], axis=0).astype(F32),
        "s5_gw": s5_glu_w[l].astype(BF16),
        "ffn_conv_w": ffn_conv_w[l].astype(F32),
        "ffn_conv_b": ffn_conv_b[l].reshape(1, 2 * D_FF).astype(F32),
    }


def _layer(x3, rt, attend, mconv_prev, ssm_h0, s5_h0_re, s5_h0_im, fconv_prev, norm1_g, norm2_g, prm, wts,
           layer, depth, kv_prev, n_valid):
    b, tp, d = x3.shape
    m = b * tp
    x2 = x3.reshape(m, d)
    w_out_all, w_up_all, w_down_all = wts
    h = _rmsnorm(x2, norm1_g, BF16)
    p2, qkv2, k_all, v_all = _inproj(h, prm["w_in"], rt, layer, depth, kv_prev)
    p3 = p2.reshape(b, tp, N_P)
    att = attend(p3, qkv2.reshape(b, tp, N_QKV))
    mam, ssm_new = _mamba(p3, mconv_prev, ssm_h0, prm, n_valid)
    s5o, s5_re, s5_im = _s5(p3, s5_h0_re, s5_h0_im, prm, n_valid)
    x2 = _outproj(att.reshape(m, ATT_WIDTH), mam.reshape(m, M_WIDTH), s5o.reshape(m, S5_WIDTH),
                  w_out_all, layer, x2)
    h2 = _rmsnorm(x2, norm2_g, BF16)
    fconv_prev = fconv_prev.astype(F32)
    if tp >= SSD_L:
        act, fconv_new = _ffn_fused(h2, w_up_all, layer, fconv_prev, prm["ffn_conv_w"], prm["ffn_conv_b"], b, tp)
    else:
        up3 = _ffn_up(h2, w_up_all, layer).reshape(b, tp, 2 * D_FF)
        act = _ffn_act_short(up3, fconv_prev, prm["ffn_conv_w"], prm["ffn_conv_b"]).reshape(m, D_FF)
        fconv_new = jnp.concatenate([fconv_prev, up3[:, :n_valid]], axis=1)[:, n_valid:]
    x2 = _ffn_down(act, w_down_all, layer, x2)

    ik = p3[:, :n_valid, C_IK:C_IK + IDX_DIM]
    xbc_raw = jnp.concatenate([mconv_prev.astype(F32), p3[:, :n_valid, C_XBC:C_XBC + M_CONV_CH]], axis=1)
    mconv_new = xbc_raw[:, n_valid:]
    return x2.reshape(b, tp, d), (k_all, v_all), (ik, ssm_new, mconv_new, s5_re, s5_im, fconv_new)


def kernel(x_prompt, x_sample, cache_k, cache_v, cache_idx_k, state_ssm, state_mconv, state_s5_re, state_s5_im,
           state_fconv, page_table, norm1_g, w_in, m_conv_w, m_conv_b, m_dt_bias, m_a_log, m_d, m_norm_g,
           s5_lam_re, s5_lam_im, s5_log_step, s5_b_re, s5_b_im, s5_c_re, s5_c_im, s5_d, s5_glu_w, s5_glu_b,
           w_out, norm2_g, ffn_w_up, ffn_conv_w, ffn_conv_b, ffn_w_down, final_norm_g):
    bp, seq, d = x_prompt.shape
    bs, t_new, _ = x_sample.shape
    depth = w_in.shape[0]
    npages = page_table.shape[1]
    past = npages * PAGE_SIZE
    tpad = SUBLANES
    assert t_new <= tpad and seq % SSD_L == 0

    rt_p = _rope_tables(jnp.arange(seq, dtype=I32), bp)
    rt_s = _rope_tables(past + jnp.arange(tpad, dtype=I32), bs)
    hp = x_prompt.astype(F32)
    hs = jnp.pad(x_sample.astype(F32), ((0, 0), (0, tpad - t_new), (0, 0)))
    n_pool = cache_k.shape[1]
    pool_k = cache_k.reshape(depth, n_pool, PAGE_SIZE * ATT_HEADS, HEAD_DIM)
    pool_v = cache_v.reshape(depth, n_pool, PAGE_SIZE * ATT_HEADS, HEAD_DIM)
    topk_p = min(TOPK_MAX, seq // 4)
    topk_s = min(TOPK_MAX, (past + t_new) // 4)

    zero_mconv = jnp.zeros((bp, M_CONV - 1, M_CONV_CH), F32)
    zero_ssm = jnp.zeros((bp, M_HEADS, M_HEAD_DIM, M_STATE), F32)
    zero_s5 = jnp.zeros((bp, S5_GROUPS, S5_STATE), F32)
    zero_fconv = jnp.zeros((bp, FFN_CONV - 1, 2 * D_FF), F32)

    pool_ik_t = cache_idx_k.transpose(0, 1, 3, 2)
    wts = (w_out.astype(BF16), ffn_w_up.astype(BF16), ffn_w_down.astype(BF16))

    def attend_prompt(p3, qkv3):
        return _prompt_attention(qkv3, _prompt_index_bias(p3, topk_p))

    new_p, new_s = [], []
    kv_p = kv_s = None
    for l in range(depth):
        prm = _prep_layer_params(l, w_in, m_conv_w, m_conv_b, m_dt_bias, m_a_log, m_d, m_norm_g,
                                 s5_lam_re, s5_lam_im, s5_log_step, s5_b_re, s5_b_im, s5_c_re, s5_c_im,
                                 s5_d, s5_glu_w, s5_glu_b, w_out, ffn_w_up, ffn_conv_w, ffn_conv_b, ffn_w_down)

        def attend_sample(p3, qkv3, l=l):
            iq = p3[:, :, C_IQ:C_IQ + IDX_HEADS * IDX_DIM].reshape(bs, tpad, IDX_HEADS, IDX_DIM)
            iq_rows = iq.transpose(0, 2, 1, 3).reshape(bs, IDX_HEADS * tpad, IDX_DIM).astype(BF16)
            iw = p3[:, :, C_SM + SM_IW:C_SM + SM_IW + IDX_HEADS] * ((IDX_HEADS ** -0.5) * (IDX_DIM ** -0.5))
            w_col = iw.transpose(0, 2, 1).reshape(bs, IDX_HEADS * tpad, 1)
            pad_rows = ((0, 0), (0, LANES - tpad), (0, 0))
            ik_new_t = jnp.pad(p3[:, :, C_IK:C_IK + IDX_DIM], pad_rows).transpose(0, 2, 1)
            mask = _sample_index_mask(page_table, iq_rows, w_col, ik_new_t, pool_ik_t, l, t_new, topk_s)
            q = qkv3[:, :, C_Q:C_Q + ATT_WIDTH].reshape(bs, tpad, ATT_HEADS, HEAD_DIM).transpose(0, 2, 1, 3)
            eye = jnp.eye(ATT_HEADS, dtype=BF16)
            qbd = (q[:, :, :, None, :] * eye[None, :, None, :, None]).reshape(bs, ATT_HEADS * tpad, ATT_WIDTH)
            k_new = jnp.pad(qkv3[:, :, C_K:C_K + ATT_WIDTH], pad_rows)
            v_new = jnp.pad(qkv3[:, :, C_V:C_V + ATT_WIDTH], pad_rows)
            return _sample_attention(page_table, qbd, mask, k_new, v_new, pool_k, pool_v, l)

        hp, kv_p, st_p = _layer(hp, rt_p, attend_prompt, zero_mconv, zero_ssm, zero_s5, zero_s5, zero_fconv,
                                norm1_g[l], norm2_g[l], prm, wts, l, depth, kv_p, seq)
        hs, kv_s, st_s = _layer(hs, rt_s, attend_sample, state_mconv[l], state_ssm[l], state_s5_re[l],
                                state_s5_im[l], state_fconv[l], norm1_g[l], norm2_g[l], prm, wts, l, depth,
                                kv_s, t_new)
        new_p.append(st_p)
        new_s.append(st_s)

    def stacked(states, i):
        return jnp.stack([s[i] for s in states], axis=0)

    def heads(kv, b, tp, n_valid):
        return kv.reshape(depth, b, tp, ATT_HEADS, HEAD_DIM)[:, :, :n_valid]

    y_prompt = _rmsnorm(hp.reshape(bp * seq, d), final_norm_g, F32).reshape(bp, seq, d)
    y_sample = _rmsnorm(hs.reshape(bs * tpad, d), final_norm_g, F32).reshape(bs, tpad, d)[:, :t_new]
    return (y_prompt, y_sample,
            heads(kv_p[0], bp, seq, seq), heads(kv_p[1], bp, seq, seq),
            stacked(new_p, 0), stacked(new_p, 1), stacked(new_p, 2), stacked(new_p, 3),
            stacked(new_p, 4), stacked(new_p, 5),
            heads(kv_s[0], bs, tpad, t_new), heads(kv_s[1], bs, tpad, t_new),
            stacked(new_s, 0), stacked(new_s, 1), stacked(new_s, 2), stacked(new_s, 3),
            stacked(new_s, 4), stacked(new_s, 5))
```

```python
import functools
import math

import numpy as np
import jax
import jax.numpy as jnp
from jax import lax
from jax.experimental import pallas as pl
from jax.experimental.pallas import tpu as pltpu

F32 = jnp.float32
BF16 = jnp.bfloat16
I32 = jnp.int32

D_MODEL = 2048
HEAD_DIM = 128
ATT_WIDTH = D_MODEL // 2
ATT_HEADS = ATT_WIDTH // HEAD_DIM
IDX_HEADS = 16
IDX_DIM = 64
TOPK_MAX = 256
ROPE_THETA = 10000.0
PAGE_SIZE = 128
M_WIDTH = D_MODEL // 4
M_HEAD_DIM = 64
M_HEADS = M_WIDTH // M_HEAD_DIM
M_GROUPS = 2
M_STATE = 128
M_CONV = 4
M_CONV_CH = M_WIDTH + 2 * M_GROUPS * M_STATE
S5_WIDTH = D_MODEL - ATT_WIDTH - M_WIDTH
S5_GROUP = 16
S5_GROUPS = S5_WIDTH // S5_GROUP
S5_STATE = 64
S5_N = S5_GROUPS * S5_STATE
D_FF = (D_MODEL * 11) // 4
FFN_CONV = 3
EPS = 1e-6
IN_SIZES = (ATT_WIDTH, ATT_WIDTH, ATT_WIDTH, IDX_HEADS * IDX_DIM, IDX_DIM, IDX_HEADS,
            M_WIDTH, M_CONV_CH, M_HEADS, S5_WIDTH)

LANES = 128
SUBLANES = 8
VMEM_LIMIT = 48 * 1024 * 1024
INPROJ_VMEM_LIMIT = 56 * 1024 * 1024

N_QKV = 3 * ATT_WIDTH
C_Q, C_K, C_V = 0, 1024, 2048
C_IQ, C_XBC, C_Z, C_U, C_IK, C_SM = 0, 1024, 2048, 2560, 3072, 3200
N_P = 3584
PROJ_TN = 512
NJ_QKV = N_QKV // PROJ_TN
SM_IW = 0
SM_DT = 16
SSD_L = 128
FFN_TM = 1024
COUNT_ACC = 4
ATT_ROWS = 128
PROJ_RC = 256
ATT_PAGES = 16
S5_KB = 256
NEG_BIG = -1e30
INT_MIN = -2 ** 31
LOG2E = math.log2(math.e)
Q_SCALE = (HEAD_DIM ** -0.5) * LOG2E
ATT_KT = 256


def _cparams(sem):
    return pltpu.CompilerParams(dimension_semantics=sem, vmem_limit_bytes=VMEM_LIMIT)


def _iota(shape, dim):
    return lax.broadcasted_iota(I32, shape, dim)


def _sigmoid(x):
    return 1.0 / (1.0 + jnp.exp(-x))


def _nt_dot(a, b):
    return lax.dot_general(a, b, (((1,), (1,)), ((), ())), preferred_element_type=F32)


def _rmsnorm_kernel(x_ref, g_ref, o_ref):
    x = x_ref[...]
    ms = jnp.mean(x * x, axis=-1, keepdims=True)
    o_ref[...] = (x * lax.rsqrt(ms + EPS) * g_ref[...]).astype(o_ref.dtype)


def _rmsnorm(x2d, g, out_dtype):
    m, d = x2d.shape
    tm = min(m, 512)
    return pl.pallas_call(
        _rmsnorm_kernel,
        grid=(m // tm,),
        in_specs=[pl.BlockSpec((tm, d), lambda i: (i, 0)), pl.BlockSpec((1, d), lambda i: (0, 0))],
        out_specs=pl.BlockSpec((tm, d), lambda i: (i, 0)),
        out_shape=jax.ShapeDtypeStruct((m, d), out_dtype),
        compiler_params=_cparams(("parallel",)),
        name="rmsnorm",
    )(x2d, g.reshape(1, d).astype(F32))


def _rope_tables(pos, batch):
    posf = pos.astype(F32)[:, None]
    half = HEAD_DIM // 2
    inv = ROPE_THETA ** (-jnp.arange(half, dtype=F32) / half)
    ang = posf * inv[None, :]
    c1, s1 = jnp.cos(ang), jnp.sin(ang)
    half2 = IDX_DIM // 2
    inv2 = ROPE_THETA ** (-jnp.arange(half2, dtype=F32) / half2)
    ang2 = posf * inv2[None, :]
    c2, s2 = jnp.cos(ang2), jnp.sin(ang2)
    tab = jnp.concatenate([c1, c1, -s1, s1, c2, c2, c2, c2, -s2, s2, -s2, s2], axis=1)
    return jnp.tile(tab, (batch, 1))


def _rope128(x, c, s):
    return x * c + pltpu.roll(x, 64, 1) * s


def _rope64(x, c, s):
    lane = _iota(x.shape, 1)
    part = jnp.where((lane & 32) == 0, pltpu.roll(x, 96, 1), pltpu.roll(x, 32, 1))
    return x * c + part * s


def _inproj_kernel(h_ref, w_ref, rt_ref, *rest):
    p_ref, qkv_ref, k4_ref, v4_ref = rest[-4:]
    j = pl.program_id(1)
    nb = PROJ_TN // LANES
    tm = h_ref.shape[0]
    rc = min(tm, PROJ_RC)

    def run(epilogue):
        for r0 in range(0, tm, rc):
            rs = slice(r0, r0 + rc)
            epilogue(r0, rs, jnp.dot(h_ref[rs, :], w_ref[...], preferred_element_type=F32))

    def head_major(dst_ref, r0, x, hb0):
        for hb in range(nb):
            dst_ref[pl.ds(r0 * ATT_HEADS + hb0 + hb, rc, stride=ATT_HEADS), :] = x[:, hb * 128:(hb + 1) * 128]

    def rope_qk(rs, acc):
        c, s = rt_ref[rs, 0:128], rt_ref[rs, 128:256]
        return jnp.concatenate([_rope128(acc[:, hb * 128:(hb + 1) * 128], c, s) for hb in range(nb)], axis=1)

    @pl.when(j < 2)
    def _():
        def q_epilogue(r0, rs, acc):
            qkv_ref[rs, :] = (rope_qk(rs, acc) * Q_SCALE).astype(BF16)
        run(q_epilogue)

    for half in range(2):
        @pl.when(j == 2 + half)
        def _():
            def k_epilogue(r0, rs, acc):
                r = rope_qk(rs, acc)
                qkv_ref[rs, :] = r.astype(BF16)
                head_major(k4_ref, r0, r, half * nb)
            run(k_epilogue)

        @pl.when(j == 4 + half)
        def _():
            def v_epilogue(r0, rs, acc):
                qkv_ref[rs, :] = acc.astype(BF16)
                head_major(v4_ref, r0, acc, half * nb)
            run(v_epilogue)

    @pl.when((j >= 6) & (j < 8))
    def _():
        def iq_epilogue(r0, rs, acc):
            c, s = rt_ref[rs, 256:384], rt_ref[rs, 384:512]
            for hb in range(nb):
                p_ref[rs, hb * 128:(hb + 1) * 128] = _rope64(acc[:, hb * 128:(hb + 1) * 128], c, s)
        run(iq_epilogue)

    @pl.when(j == 12)
    def _():
        def tail_epilogue(r0, rs, acc):
            c, s = rt_ref[rs, 256:384], rt_ref[rs, 384:512]
            p_ref[rs, 0:128] = _rope64(acc[:, 0:128], c, s)
            p_ref[rs, 128:] = acc[:, 128:]
        run(tail_epilogue)

    @pl.when((j >= 8) & (j < 12))
    def _():
        def plain_epilogue(r0, rs, acc):
            p_ref[rs, :] = acc
        run(plain_epilogue)


def _inproj(h, w, rt, layer, depth, kv_prev):
    m, d = h.shape
    tm = min(m, 1024)
    nj = w.shape[1] // PROJ_TN
    kv_spec = pl.BlockSpec((None, tm * ATT_HEADS, HEAD_DIM), lambda i, j: (layer, i, 0))
    kv_shape = jax.ShapeDtypeStruct((depth, m * ATT_HEADS, HEAD_DIM), F32)
    in_specs = [pl.BlockSpec((tm, d), lambda i, j: (i, 0)),
                pl.BlockSpec((d, PROJ_TN), lambda i, j: (0, j)),
                pl.BlockSpec((tm, 512), lambda i, j: (i, 0))]
    args = [h, w, rt]
    aliases = {}
    if kv_prev is not None:
        in_specs += [pl.BlockSpec(memory_space=pl.ANY)] * 2
        args += list(kv_prev)
        aliases = {3: 2, 4: 3}
    return pl.pallas_call(
        _inproj_kernel,
        grid=(m // tm, nj),
        in_specs=in_specs,
        out_specs=[pl.BlockSpec((tm, PROJ_TN), lambda i, j: (i, jnp.maximum(j - NJ_QKV, 0))),
                   pl.BlockSpec((tm, PROJ_TN), lambda i, j: (i, jnp.minimum(j, NJ_QKV - 1))),
                   kv_spec, kv_spec],
        out_shape=[jax.ShapeDtypeStruct((m, N_P), F32), jax.ShapeDtypeStruct((m, N_QKV), BF16),
                   kv_shape, kv_shape],
        input_output_aliases=aliases,
        compiler_params=pltpu.CompilerParams(dimension_semantics=("parallel", "arbitrary"),
                                             vmem_limit_bytes=INPROJ_VMEM_LIMIT),
        name="inproj",
    )(*args)


def _prep_w_in(w):
    parts, start = [], 0
    for sz in IN_SIZES:
        parts.append(w[:, start:start + sz])
        start += sz
    q, k, v, iq, ik, iw, z, xbc, dt, u = parts
    d = w.shape[0]
    small = jnp.concatenate([iw, dt, jnp.zeros((d, LANES - IDX_HEADS - M_HEADS), w.dtype)], axis=1)
    tail = jnp.zeros((d, N_P - C_SM - LANES), w.dtype)
    return jnp.concatenate([q, k, v, iq, xbc, z, u, ik, ik, small, tail], axis=1).astype(BF16)


def _sort_key(x):
    bits = pltpu.bitcast(x, I32)
    return jnp.where(bits < 0, bits ^ jnp.int32(0x7FFFFFFF), bits)


def _radix_select(count_fn, k_eff, shape, idx_bits, n_total):
    kf = k_eff.astype(F32)

    def bit_body(it, carry):
        t, cnt_t = carry
        cand = t + (jnp.int32(1) << (31 - it))
        cnt = count_fn(lambda key, idx: jnp.where(key >= cand, 1.0, 0.0))
        take = cnt >= kf
        return jnp.where(take, cand, t), jnp.where(take, cnt, cnt_t)

    t, cnt_t = lax.fori_loop(0, 32, bit_body,
                             (jnp.full(shape, INT_MIN, I32), jnp.zeros(shape, F32) + n_total))

    def with_ties():
        need = kf - count_fn(lambda key, idx: jnp.where(key > t, 1.0, 0.0))

        def tie_body(it, c):
            cand = c + (jnp.int32(1) << (idx_bits - 1 - it))
            cnt = count_fn(lambda key, idx: jnp.where(key == t, jnp.where(idx < cand, 1.0, 0.0), 0.0))
            return jnp.where(cnt < need, cand, c)

        return lax.fori_loop(0, idx_bits, tie_body, jnp.zeros(shape, I32))

    c = lax.cond(jnp.max(cnt_t - kf) > 0.0, with_ties, lambda: jnp.full(shape, (1 << idx_bits) - 1, I32))
    return t, c


def _selected(key, idx, t, c):
    return jnp.where(key > t, 1.0, jnp.where(key == t, jnp.where(idx <= c, 1.0, 0.0), 0.0))


def _pidx_kernel(iq_ref, ik_ref, wt_ref, bias_ref, key_sc, iqb_sc, *, tq, tk, topk, seq):
    i = pl.program_id(1)
    n_chunks = ((i + 1) * tq + tk - 1) // tk
    iqb_sc[...] = iq_ref[...].astype(BF16)
    wt = wt_ref[...] * ((IDX_HEADS ** -0.5) * (IDX_DIM ** -0.5))
    qpos = i * tq + _iota((1, tq), 1)
    lane_k = _iota((tk, LANES), 1)
    bias_ref[...] = jnp.full(bias_ref.shape, NEG_BIG, bias_ref.dtype)

    def score_chunk(c, carry):
        off = pl.multiple_of(c * tk, tk)
        ik = ik_ref[pl.ds(off, tk), :]
        ik_e = jnp.where(lane_k < IDX_DIM, ik, 0.0).astype(BF16)
        ik_o = jnp.where(lane_k >= IDX_DIM, ik, 0.0).astype(BF16)
        acc = jnp.zeros((tk, tq), F32)
        for p in range(IDX_HEADS // 2):
            qp = iqb_sc[:, p * 128:(p + 1) * 128]
            acc = acc + wt[2 * p:2 * p + 1, :] * jnp.maximum(_nt_dot(ik_e, qp), 0.0)
            acc = acc + wt[2 * p + 1:2 * p + 2, :] * jnp.maximum(_nt_dot(ik_o, qp), 0.0)
        kpos = off + _iota((tk, tq), 0)
        acc = jnp.where(kpos <= qpos, acc, -jnp.inf)
        key_sc[pl.ds(off, tk), :] = _sort_key(acc)
        return carry

    lax.fori_loop(0, n_chunks, score_chunk, 0)
    acc_rows = COUNT_ACC * SUBLANES

    def count_fn(pred):
        def body(c, part):
            off = pl.multiple_of(c * tk, tk)
            key = key_sc[pl.ds(off, tk), :]
            idx = off + _iota((tk, tq), 0)
            m = pred(key, idx)
            return part + jnp.sum(m.reshape(tk // acc_rows, acc_rows, tq), axis=0)
        part = lax.fori_loop(0, n_chunks, body, jnp.zeros((acc_rows, tq), F32))
        return jnp.sum(part, axis=0, keepdims=True)

    k_eff = jnp.minimum(topk, qpos + 1)
    t, c = _radix_select(count_fn, k_eff, (1, tq), max(1, (seq - 1).bit_length()),
                         (n_chunks * tk).astype(F32))

    def write_chunk(cc, carry):
        off = pl.multiple_of(cc * tk, tk)
        key = key_sc[pl.ds(off, tk), :]
        idx = off + _iota((tk, tq), 0)
        bias_t = (_selected(key, idx, t, c) - 1.0) * (-NEG_BIG)
        bias_ref[:, pl.ds(off, tk)] = bias_t.T.astype(bias_ref.dtype)
        return carry

    lax.fori_loop(0, n_chunks, write_chunk, 0)


def _prompt_index_bias(p3, topk):
    b, s, _ = p3.shape
    tq = min(s, 256)
    tk = min(s, 512)
    wt = p3[:, :, C_SM + SM_IW:C_SM + SM_IW + IDX_HEADS].transpose(0, 2, 1)
    kern = functools.partial(_pidx_kernel, tq=tq, tk=tk, topk=topk, seq=s)
    return pl.pallas_call(
        kern,
        grid=(b, s // tq),
        in_specs=[pl.BlockSpec((None, tq, 1024), lambda bb, i: (bb, i, C_IQ // 1024)),
                  pl.BlockSpec((None, s, LANES), lambda bb, i: (bb, 0, C_IK // LANES)),
                  pl.BlockSpec((None, IDX_HEADS, tq), lambda bb, i: (bb, 0, i))],
        out_specs=pl.BlockSpec((None, tq, s), lambda bb, i: (bb, i, 0)),
        out_shape=jax.ShapeDtypeStruct((b, s, s), BF16),
        scratch_shapes=[pltpu.VMEM((s, tq), I32), pltpu.VMEM((tq, 1024), BF16)],
        compiler_params=_cparams(("parallel", "arbitrary")),
        name="prompt_index_bias",
    )(p3, p3, wt)


def _pattn_kernel(qi_ref, ki_ref, q_ref, k_ref, v_ref, b_ref, o_ref, m_sc, l_sc, acc_sc, *, tq, tk):
    st = pl.program_id(1)
    i = qi_ref[st]
    kc = ki_ref[st]
    last = ((i + 1) * tq - 1) // tk
    kt = min(tk, ATT_KT)
    rep = kt // LANES

    @pl.when(kc == 0)
    def _():
        m_sc[...] = jnp.full(m_sc.shape, NEG_BIG, F32)
        l_sc[...] = jnp.zeros(l_sc.shape, F32)
        acc_sc[...] = jnp.zeros(acc_sc.shape, F32)

    ones = jnp.ones((kt, HEAD_DIM), BF16)
    tr = min(tq, ATT_ROWS)
    for r0 in range(0, tq, tr):
        rs = slice(r0, r0 + tr)
        for k0 in range(0, tk, kt):
            ks = slice(k0, k0 + kt)
            bias = b_ref[rs, ks].astype(F32)
            for h in range(ATT_HEADS):
                sl = slice(h * HEAD_DIM, (h + 1) * HEAD_DIM)
                s = _nt_dot(q_ref[rs, sl], k_ref[ks, sl]) + bias
                m_old = m_sc[rs, sl]
                m_new = jnp.maximum(m_old, jnp.max(s, axis=1, keepdims=True))
                alpha = jnp.exp2(m_old - m_new)
                p = jnp.exp2(s - jnp.concatenate([m_new] * rep, axis=1)).astype(BF16)
                pv = jnp.dot(p, jnp.concatenate([v_ref[ks, sl], ones], axis=1), preferred_element_type=F32)
                acc_sc[rs, sl] = alpha * acc_sc[rs, sl] + pv[:, 0:HEAD_DIM]
                l_sc[rs, sl] = alpha * l_sc[rs, sl] + pv[:, HEAD_DIM:]
                m_sc[rs, sl] = m_new

    @pl.when(kc == last)
    def _():
        o_ref[...] = (acc_sc[...] / l_sc[...]).astype(o_ref.dtype)


def _prompt_attention(qkv3, bias):
    b, s, _ = qkv3.shape
    tq = min(s, 256)
    tk = min(s, 512)
    steps = [(i, kc) for i in range(s // tq) for kc in range(((i + 1) * tq - 1) // tk + 1)]
    qi = jnp.asarray(np.array([a for a, _ in steps], np.int32))
    ki = jnp.asarray(np.array([c for _, c in steps], np.int32))
    kern = functools.partial(_pattn_kernel, tq=tq, tk=tk)
    grid_spec = pltpu.PrefetchScalarGridSpec(
        num_scalar_prefetch=2,
        grid=(b, len(steps)),
        in_specs=[pl.BlockSpec((None, tq, ATT_WIDTH), lambda bb, st, qi, ki: (bb, qi[st], C_Q // ATT_WIDTH)),
                  pl.BlockSpec((None, tk, ATT_WIDTH), lambda bb, st, qi, ki: (bb, ki[st], C_K // ATT_WIDTH)),
                  pl.BlockSpec((None, tk, ATT_WIDTH), lambda bb, st, qi, ki: (bb, ki[st], C_V // ATT_WIDTH)),
                  pl.BlockSpec((None, tq, tk), lambda bb, st, qi, ki: (bb, qi[st], ki[st]))],
        out_specs=pl.BlockSpec((None, tq, ATT_WIDTH), lambda bb, st, qi, ki: (bb, qi[st], 0)),
        scratch_shapes=[pltpu.VMEM((tq, ATT_WIDTH), F32)] * 3,
    )
    return pl.pallas_call(
        kern,
        grid_spec=grid_spec,
        out_shape=jax.ShapeDtypeStruct((b, s, ATT_WIDTH), BF16),
        compiler_params=_cparams(("parallel", "arbitrary")),
        name="prompt_attention",
    )(qi, ki, qkv3, qkv3, qkv3, bias)


def _sidx_kernel(pt_ref, iq_ref, w_ref, iknew_ref, *rest, pg, nsteps, past, n_new, topk):
    pages = rest[:pg]
    mask_ref = rest[pg]
    key_sc = rest[pg + 1]
    st = pl.program_id(1)
    iq = iq_ref[...]
    w = w_ref[...]
    width = past + LANES

    def score(ikp_t):
        d = jnp.maximum(jnp.dot(iq, ikp_t.astype(BF16), preferred_element_type=F32), 0.0) * w
        acc = d[0:SUBLANES]
        for h in range(1, IDX_HEADS):
            acc = acc + d[h * SUBLANES:(h + 1) * SUBLANES]
        return acc

    for r in range(pg):
        off = pl.multiple_of((st * pg + r) * PAGE_SIZE, PAGE_SIZE)
        key_sc[:, pl.ds(off, PAGE_SIZE)] = _sort_key(score(pages[r][...]))

    @pl.when(st == nsteps - 1)
    def _():
        trow = _iota((SUBLANES, LANES), 0)
        jcol = _iota((SUBLANES, LANES), 1)
        sc = score(iknew_ref[...])
        sc = jnp.where(jcol <= trow, jnp.where(jcol < n_new, sc, -jnp.inf), -jnp.inf)
        key_sc[:, past:width] = _sort_key(sc)
        key = key_sc[...]
        idx = _iota((SUBLANES, width), 1)

        def count_fn(pred):
            return jnp.sum(pred(key, idx), axis=1, keepdims=True)

        qpos = past + _iota((SUBLANES, 1), 0)
        k_eff = jnp.minimum(topk, qpos + 1)
        t, c = _radix_select(count_fn, k_eff, (SUBLANES, 1), max(1, (width - 1).bit_length()), float(width))
        mask_ref[...] = _selected(key, idx, t, c)


def _sample_index_mask(page_table, iq_rows, w_col, ik_new, pool_ik, layer, n_new, topk):
    b, npages = page_table.shape
    pg = 8 if npages % 8 == 0 else npages
    nsteps = npages // pg
    past = npages * PAGE_SIZE
    width = past + LANES

    def page_spec(r):
        return pl.BlockSpec((None, None, IDX_DIM, PAGE_SIZE),
                            lambda bb, st, pt: (layer, pt[bb, st * pg + r], 0, 0))

    kern = functools.partial(_sidx_kernel, pg=pg, nsteps=nsteps, past=past, n_new=n_new, topk=topk)
    grid_spec = pltpu.PrefetchScalarGridSpec(
        num_scalar_prefetch=1,
        grid=(b, nsteps),
        in_specs=[pl.BlockSpec((None, IDX_HEADS * SUBLANES, IDX_DIM), lambda bb, st, pt: (bb, 0, 0)),
                  pl.BlockSpec((None, IDX_HEADS * SUBLANES, 1), lambda bb, st, pt: (bb, 0, 0)),
                  pl.BlockSpec((None, IDX_DIM, LANES), lambda bb, st, pt: (bb, 0, 0))]
                 + [page_spec(r) for r in range(pg)],
        out_specs=pl.BlockSpec((None, SUBLANES, width), lambda bb, st, pt: (bb, 0, 0)),
        scratch_shapes=[pltpu.VMEM((SUBLANES, width), I32)],
    )
    return pl.pallas_call(
        kern,
        grid_spec=grid_spec,
        out_shape=jax.ShapeDtypeStruct((b, SUBLANES, width), F32),
        compiler_params=_cparams(("parallel", "arbitrary")),
        name="sample_index_mask",
    )(page_table, iq_rows, w_col, ik_new, *([pool_ik] * pg))


def _sattn_kernel(pt_ref, qbd_ref, mask_ref, knew_ref, vnew_ref, *rest, pg, nsteps, past):
    kpages = rest[:pg]
    vpages = rest[pg:2 * pg]
    o_ref = rest[2 * pg]
    m_sc, l_sc, acc_sc = rest[2 * pg + 1:]
    st = pl.program_id(1)

    @pl.when(st == 0)
    def _():
        m_sc[...] = jnp.full(m_sc.shape, NEG_BIG, F32)
        l_sc[...] = jnp.zeros(l_sc.shape, F32)
        acc_sc[...] = jnp.zeros(acc_sc.shape, F32)

    qbd = qbd_ref[...]

    def page2d(ref):
        return jnp.concatenate([ref[pl.ds(h, PAGE_SIZE, stride=ATT_HEADS), :] for h in range(ATT_HEADS)],
                               axis=1).astype(BF16)

    def update(k_list, v_list, mk8):
        n = len(k_list)
        mk = jnp.concatenate([mk8] * ATT_HEADS, axis=0) > 0
        s = jnp.concatenate([_nt_dot(qbd, k) for k in k_list], axis=1)
        s = jnp.where(mk, s, NEG_BIG)
        m_old = m_sc[...]
        m_new = jnp.maximum(m_old, jnp.max(s, axis=1, keepdims=True))
        alpha = jnp.exp2(m_old - m_new)
        p = jnp.where(mk, jnp.exp2(s - jnp.concatenate([m_new] * n, axis=1)), 0.0)
        l_sc[...] = alpha * l_sc[...] + jnp.sum(p, axis=1, keepdims=True)
        pb = p.astype(BF16)
        pv = jnp.dot(pb[:, 0:LANES], v_list[0], preferred_element_type=F32)
        for r in range(1, n):
            pv = pv + jnp.dot(pb[:, r * LANES:(r + 1) * LANES], v_list[r], preferred_element_type=F32)
        acc_sc[...] = jnp.concatenate([alpha] * ATT_HEADS, axis=1) * acc_sc[...] + pv
        m_sc[...] = m_new

    off = pl.multiple_of(st * pg * PAGE_SIZE, pg * PAGE_SIZE)
    update([page2d(r) for r in kpages], [page2d(r) for r in vpages], mask_ref[:, pl.ds(off, pg * PAGE_SIZE)])

    @pl.when(st == nsteps - 1)
    def _():
        update([knew_ref[...]], [vnew_ref[...]], mask_ref[:, past:past + LANES])
        for h in range(ATT_HEADS):
            rs = slice(h * SUBLANES, (h + 1) * SUBLANES)
            cs = slice(h * HEAD_DIM, (h + 1) * HEAD_DIM)
            o_ref[:, cs] = (acc_sc[rs, cs] / l_sc[rs, :]).astype(o_ref.dtype)


def _sample_attention(page_table, qbd, mask, k_new, v_new, pool_k, pool_v, layer):
    b, npages = page_table.shape
    pg = ATT_PAGES if npages % ATT_PAGES == 0 else npages
    nsteps = npages // pg
    past = npages * PAGE_SIZE
    width = past + LANES
    rows = ATT_HEADS * SUBLANES

    def page_spec(r):
        return pl.BlockSpec((None, None, PAGE_SIZE * ATT_HEADS, HEAD_DIM),
                            lambda bb, st, pt: (layer, pt[bb, st * pg + r], 0, 0))

    kern = functools.partial(_sattn_kernel, pg=pg, nsteps=nsteps, past=past)
    grid_spec = pltpu.PrefetchScalarGridSpec(
        num_scalar_prefetch=1,
        grid=(b, nsteps),
        in_specs=[pl.BlockSpec((None, rows, ATT_WIDTH), lambda bb, st, pt: (bb, 0, 0)),
                  pl.BlockSpec((None, SUBLANES, width), lambda bb, st, pt: (bb, 0, 0)),
                  pl.BlockSpec((None, LANES, ATT_WIDTH), lambda bb, st, pt: (bb, 0, 0)),
                  pl.BlockSpec((None, LANES, ATT_WIDTH), lambda bb, st, pt: (bb, 0, 0))]
                 + [page_spec(r) for r in range(pg)] * 2,
        out_specs=pl.BlockSpec((None, SUBLANES, ATT_WIDTH), lambda bb, st, pt: (bb, 0, 0)),
        scratch_shapes=[pltpu.VMEM((rows, LANES), F32), pltpu.VMEM((rows, LANES), F32),
                        pltpu.VMEM((rows, ATT_WIDTH), F32)],
    )
    return pl.pallas_call(
        kern,
        grid_spec=grid_spec,
        out_shape=jax.ShapeDtypeStruct((b, SUBLANES, ATT_WIDTH), BF16),
        compiler_params=_cparams(("parallel", "arbitrary")),
        name="sample_attention",
    )(page_table, qbd, mask, k_new, v_new, *([pool_k] * pg), *([pool_v] * pg))


def _shift_rows(x, prev, d):
    r = pltpu.roll(x, d, 0)
    rowi = _iota(x.shape, 0)
    np_ = prev.shape[0]
    for q in range(d):
        r = jnp.where(rowi == q, prev[np_ - d + q:np_ - d + q + 1, :], r)
    return r


def _per_head(v):
    lane = _iota((v.shape[0], LANES), 1)
    outs = []
    for j in range(M_HEADS // 2):
        a = v[:, SM_DT + 2 * j:SM_DT + 2 * j + 1]
        b = v[:, SM_DT + 2 * j + 1:SM_DT + 2 * j + 2]
        outs.append(jnp.where(lane < M_HEAD_DIM, a, b))
    return jnp.concatenate(outs, axis=1)


def _mamba_kernel(xbc_ref, halo_ref, z_ref, sm_ref, cprev_ref, h0_ref, cw_ref, cb_ref, hp_ref,
                  dexp_ref, ng_ref, y_ref, hout_ref, h_sc, *, rows, n_valid, nsteps):
    c = pl.program_id(1)
    L = SSD_L
    hi = lax.Precision.HIGHEST

    @pl.when(c == 0)
    def _():
        h_sc[...] = h0_ref[...]

    x = xbc_ref[...]
    prev = jnp.where(c == 0, cprev_ref[...], halo_ref[SUBLANES - (M_CONV - 1):SUBLANES, :])
    conv = cb_ref[...] + x * cw_ref[M_CONV - 1:M_CONV, :]
    for d in range(1, M_CONV):
        conv = conv + _shift_rows(x, prev, d) * cw_ref[M_CONV - 1 - d:M_CONV - d, :]
    xc = conv * _sigmoid(conv)

    smv = sm_ref[...] + hp_ref[0:1, :]
    dt = jnp.maximum(smv, 0.0) + jnp.log1p(jnp.exp(-jnp.abs(smv)))
    rowg = c * rows + _iota((rows, LANES), 0)
    dt = jnp.where(rowg < n_valid, dt, 0.0)
    zz = z_ref[...]
    if rows < L:
        xc = jnp.concatenate([xc, jnp.zeros((L - rows, xc.shape[1]), F32)], axis=0)
        dt = jnp.concatenate([dt, jnp.zeros((L - rows, LANES), F32)], axis=0)
        zz = jnp.concatenate([zz, jnp.zeros((L - rows, zz.shape[1]), F32)], axis=0)

    xs = xc[:, 0:M_WIDTH]
    bm = xc[:, M_WIDTH:M_WIDTH + M_GROUPS * M_STATE].astype(BF16)
    cm = xc[:, M_WIDTH + M_GROUPS * M_STATE:].astype(BF16)

    da = dt * (-jnp.exp(hp_ref[1:2, :]))
    ri = _iota((L, L), 0)
    ci = _iota((L, L), 1)
    causal = ri >= ci
    acum = jnp.dot(jnp.where(causal, 1.0, 0.0), da, precision=hi, preferred_element_type=F32)
    acum_t = acum.T
    alast = acum[L - 1:L, :]

    xdt = xs * _per_head(dt)
    xw = xdt * _per_head(jnp.exp(alast - acum))
    xw_t = jnp.concatenate([xw[:, j * LANES:(j + 1) * LANES].T for j in range(M_WIDTH // LANES)], axis=0)
    h_in = h_sc[...]
    gw = M_WIDTH // M_GROUPS

    y_off, states, cb = [], [], []
    for g in range(M_GROUPS):
        bg = bm[:, g * M_STATE:(g + 1) * M_STATE]
        cg = cm[:, g * M_STATE:(g + 1) * M_STATE]
        states.append(jnp.dot(xw_t[g * gw:(g + 1) * gw, :].astype(BF16), bg, preferred_element_type=F32))
        y_off.append(_nt_dot(cg, h_in[g * gw:(g + 1) * gw, :].astype(BF16)))
        cb.append(_nt_dot(cg, bg))
    y = jnp.concatenate(y_off, axis=1) * _per_head(jnp.exp(acum))

    lane = _iota((L, LANES), 1)
    y_diag = []
    for j in range(M_HEADS // 2):
        xpair = xdt[:, j * LANES:(j + 1) * LANES]
        acc = jnp.zeros((L, LANES), F32)
        for e in range(2):
            h = 2 * j + e
            seg = acum[:, SM_DT + h:SM_DT + h + 1] - acum_t[SM_DT + h:SM_DT + h + 1, :]
            mh = (cb[h // (M_HEADS // M_GROUPS)] * jnp.exp(jnp.where(causal, seg, -jnp.inf))).astype(BF16)
            keep = (lane < M_HEAD_DIM) if e == 0 else (lane >= M_HEAD_DIM)
            acc = acc + jnp.dot(mh, jnp.where(keep, xpair, 0.0).astype(BF16), preferred_element_type=F32)
        y_diag.append(acc)
    y = y + jnp.concatenate(y_diag, axis=1) + xs * dexp_ref[...]

    cdec = jnp.concatenate(
        [jnp.broadcast_to(acum_t[SM_DT + h:SM_DT + h + 1, L - 1:L], (M_HEAD_DIM, M_STATE))
         for h in range(M_HEADS)], axis=0)
    h_new = h_in * jnp.exp(cdec) + jnp.concatenate(states, axis=0)
    h_sc[...] = h_new

    @pl.when(c == nsteps - 1)
    def _():
        hout_ref[...] = h_new

    y = y * (zz * _sigmoid(zz))
    outs = []
    for g in range(M_GROUPS):
        seg = y[:, g * gw:(g + 1) * gw]
        ms = jnp.mean(seg * seg, axis=-1, keepdims=True)
        outs.append(seg * lax.rsqrt(ms + EPS) * ng_ref[:, g * gw:(g + 1) * gw])
    y_ref[...] = jnp.concatenate(outs, axis=1)[0:rows].astype(y_ref.dtype)


def _mamba(p3, conv_prev, h0, prm, n_valid):
    b, tp, _ = p3.shape
    rows = min(tp, SSD_L)
    nsteps = tp // rows
    hb = rows // SUBLANES
    kern = functools.partial(_mamba_kernel, rows=rows, n_valid=n_valid, nsteps=nsteps)
    full2 = lambda shape: pl.BlockSpec(shape, lambda bb, c: (0, 0))
    y, hout = pl.pallas_call(
        kern,
        grid=(b, nsteps),
        in_specs=[pl.BlockSpec((None, rows, M_CONV_CH), lambda bb, c: (bb, c, C_XBC // M_CONV_CH)),
                  pl.BlockSpec((None, SUBLANES, M_CONV_CH),
                               lambda bb, c: (bb, jnp.maximum(c * hb - 1, 0), C_XBC // M_CONV_CH)),
                  pl.BlockSpec((None, rows, M_WIDTH), lambda bb, c: (bb, c, C_Z // M_WIDTH)),
                  pl.BlockSpec((None, rows, LANES), lambda bb, c: (bb, c, C_SM // LANES)),
                  pl.BlockSpec((None, M_CONV - 1, M_CONV_CH), lambda bb, c: (bb, 0, 0)),
                  pl.BlockSpec((None, M_WIDTH, M_STATE), lambda bb, c: (bb, 0, 0)),
                  full2((M_CONV, M_CONV_CH)), full2((1, M_CONV_CH)), full2((2, LANES)),
                  full2((1, M_WIDTH)), full2((1, M_WIDTH))],
        out_specs=[pl.BlockSpec((None, rows, M_WIDTH), lambda bb, c: (bb, c, 0)),
                   pl.BlockSpec((None, M_WIDTH, M_STATE), lambda bb, c: (bb, 0, 0))],
        out_shape=[jax.ShapeDtypeStruct((b, tp, M_WIDTH), BF16),
                   jax.ShapeDtypeStruct((b, M_WIDTH, M_STATE), F32)],
        scratch_shapes=[pltpu.VMEM((M_WIDTH, M_STATE), F32)],
        compiler_params=_cparams(("parallel", "arbitrary")),
        name="mamba",
    )(p3, p3, p3, p3, conv_prev.astype(F32), h0.reshape(b, M_WIDTH, M_STATE).astype(F32),
      prm["m_conv_w"], prm["m_conv_b"], prm["m_hp"], prm["m_dexp"], prm["m_ng"])
    return y, hout.reshape(b, M_HEADS, M_HEAD_DIM, M_STATE)


def _gelu_tanh(x):
    return 0.5 * x * (1.0 + jnp.tanh(math.sqrt(2.0 / math.pi) * (x + 0.044715 * (x * x * x))))


def _s5_kernel(u_ref, h0r_ref, h0i_ref, lam_ref, bre_ref, bim_ref, cre_ref, cim_ref, dg_ref, gw_ref,
               o_ref, sr_ref, si_ref, cr_sc, ci_sc, *, rows, n_valid):
    t = pl.program_id(1)

    @pl.when(t == 0)
    def _():
        cr_sc[...] = h0r_ref[...]
        ci_sc[...] = h0i_ref[...]

    lr, li = lam_ref[0:1, :], lam_ref[1:2, :]
    step = jnp.exp(lam_ref[2:3, :])
    mag = jnp.exp(lr * step)
    ar, ai = mag * jnp.cos(li * step), mag * jnp.sin(li * step)
    den = lr * lr + li * li
    gr = ((ar - 1.0) * lr + ai * li) / den
    gi = (ai * lr - (ar - 1.0) * li) / den

    nslab = S5_WIDTH // S5_KB
    nb = S5_N // nslab

    def in_proj(x, w_ref):
        return jnp.concatenate(
            [jnp.dot(x[:, k * S5_KB:(k + 1) * S5_KB], w_ref[k * S5_KB:(k + 1) * S5_KB, k * nb:(k + 1) * nb],
                     preferred_element_type=F32) for k in range(nslab)], axis=1)

    def out_proj(x, w_ref):
        return jnp.concatenate(
            [jnp.dot(x[:, k * nb:(k + 1) * nb], w_ref[k * nb:(k + 1) * nb, k * S5_KB:(k + 1) * S5_KB],
                     preferred_element_type=F32) for k in range(nslab)], axis=1)

    u = u_ref[...]
    ub = u.astype(BF16)
    bur = in_proj(ub, bre_ref)
    bui = in_proj(ub, bim_ref)

    seg = rows // SUBLANES
    shape8 = (SUBLANES, S5_N)
    sub = _iota(shape8, 0)
    a8r, a8i = jnp.broadcast_to(ar, shape8), jnp.broadcast_to(ai, shape8)
    g8r, g8i = jnp.broadcast_to(gr, shape8), jnp.broadcast_to(gi, shape8)

    hr_t, hi_t = [], []
    for j in range(seg):
        xr = bur[j * SUBLANES:(j + 1) * SUBLANES, :]
        xi = bui[j * SUBLANES:(j + 1) * SUBLANES, :]
        tr = g8r * xr - g8i * xi
        ti = g8r * xi + g8i * xr
        if j > 0:
            tr, ti = tr + a8r * hr_t[-1] - a8i * hi_t[-1], ti + a8r * hi_t[-1] + a8i * hr_t[-1]
        hr_t.append(tr)
        hi_t.append(ti)

    mr, mi = a8r, a8i
    n = 1
    while n < seg:
        mr, mi = mr * mr - mi * mi, 2.0 * mr * mi
        n *= 2
    c0r = jnp.broadcast_to(cr_sc[...], shape8)
    c0i = jnp.broadcast_to(ci_sc[...], shape8)
    er = hr_t[-1] + jnp.where(sub == 0, mr * c0r - mi * c0i, 0.0)
    ei = hi_t[-1] + jnp.where(sub == 0, mr * c0i + mi * c0r, 0.0)
    d = 1
    while d < SUBLANES:
        sr = jnp.where(sub >= d, pltpu.roll(er, d, 0), 0.0)
        si = jnp.where(sub >= d, pltpu.roll(ei, d, 0), 0.0)
        er, ei = er + mr * sr - mi * si, ei + mr * si + mi * sr
        mr, mi = mr * mr - mi * mi, 2.0 * mr * mi
        d *= 2
    wr = jnp.where(sub == 0, c0r, pltpu.roll(er, 1, 0))
    wi = jnp.where(sub == 0, c0i, pltpu.roll(ei, 1, 0))
    for j in range(seg):
        wr, wi = a8r * wr - a8i * wi, a8r * wi + a8i * wr
        hr_t[j] = hr_t[j] + wr
        hi_t[j] = hi_t[j] + wi

    cr_sc[...] = er[SUBLANES - 1:SUBLANES, :]
    ci_sc[...] = ei[SUBLANES - 1:SUBLANES, :]
    r_last = (n_valid - 1) % rows
    s_last, j_last = r_last // seg, r_last % seg

    @pl.when(t == (n_valid - 1) // rows)
    def _():
        sr_ref[...] = hr_t[j_last][s_last:s_last + 1, :]
        si_ref[...] = hi_t[j_last][s_last:s_last + 1, :]

    hr = jnp.concatenate(hr_t, axis=0)
    hi = jnp.concatenate(hi_t, axis=0)
    y = out_proj(hr.astype(BF16), cre_ref) - out_proj(hi.astype(BF16), cim_ref) + dg_ref[0:1, :] * u
    g = _gelu_tanh(y)
    gate = jnp.dot(g.astype(BF16), gw_ref[...], preferred_element_type=F32) + dg_ref[1:2, :]
    o_ref[...] = (g * _sigmoid(gate)).astype(o_ref.dtype)


def _s5(p3, h0_re, h0_im, prm, n_valid):
    b, tp, _ = p3.shape
    rows = min(tp, 256)
    nsteps = tp // rows
    seg = rows // SUBLANES

    def permute(x, a, c):
        return x.reshape(b, nsteps, a, c, S5_WIDTH).swapaxes(2, 3).reshape(b, tp, S5_WIDTH)

    u3 = p3[:, :, C_U:C_U + S5_WIDTH]
    if seg > 1:
        u3 = permute(u3, SUBLANES, seg)
    kern = functools.partial(_s5_kernel, rows=rows, n_valid=n_valid)
    full2 = lambda shape: pl.BlockSpec(shape, lambda bb, t: (0, 0))
    state_spec = pl.BlockSpec((None, 1, S5_N), lambda bb, t: (bb, 0, 0))
    out, sr, si = pl.pallas_call(
        kern,
        grid=(b, nsteps),
        in_specs=[pl.BlockSpec((None, rows, S5_WIDTH), lambda bb, t: (bb, t, 0)),
                  state_spec, state_spec, full2((3, S5_N)),
                  full2((S5_WIDTH, S5_N)), full2((S5_WIDTH, S5_N)),
                  full2((S5_N, S5_WIDTH)), full2((S5_N, S5_WIDTH)),
                  full2((2, S5_WIDTH)), full2((S5_WIDTH, S5_WIDTH))],
        out_specs=[pl.BlockSpec((None, rows, S5_WIDTH), lambda bb, t: (bb, t, 0)), state_spec, state_spec],
        out_shape=[jax.ShapeDtypeStruct((b, tp, S5_WIDTH), BF16),
                   jax.ShapeDtypeStruct((b, 1, S5_N), F32),
                   jax.ShapeDtypeStruct((b, 1, S5_N), F32)],
        scratch_shapes=[pltpu.VMEM((1, S5_N), F32), pltpu.VMEM((1, S5_N), F32)],
        compiler_params=_cparams(("parallel", "arbitrary")),
        name="s5",
    )(u3, h0_re.reshape(b, 1, S5_N).astype(F32), h0_im.reshape(b, 1, S5_N).astype(F32),
      prm["s5_lam"], prm["s5_bre"], prm["s5_bim"], prm["s5_cre"], prm["s5_cim"], prm["s5_dg"], prm["s5_gw"])
    if seg > 1:
        out = permute(out, seg, SUBLANES)
    return out, sr.reshape(b, S5_GROUPS, S5_STATE), si.reshape(b, S5_GROUPS, S5_STATE)


def _block_diag_in(w):
    g, n, k = w.shape
    eye = jnp.eye(g, dtype=w.dtype)
    return jnp.einsum('gnk,gh->gkhn', w, eye).reshape(g * k, g * n)


def _block_diag_out(w):
    g, k, n = w.shape
    eye = jnp.eye(g, dtype=w.dtype)
    return jnp.einsum('gkn,gh->gnhk', w, eye).reshape(g * n, g * k)


def _outproj_kernel(att_ref, mam_ref, s5_ref, w_ref, res_ref, g_ref, o_ref, hn_ref):
    acc = jnp.dot(att_ref[...], w_ref[0:ATT_WIDTH, :], preferred_element_type=F32)
    acc = acc + jnp.dot(mam_ref[...], w_ref[ATT_WIDTH:ATT_WIDTH + M_WIDTH, :], preferred_element_type=F32)
    acc = acc + jnp.dot(s5_ref[...], w_ref[ATT_WIDTH + M_WIDTH:, :], preferred_element_type=F32)
    x = res_ref[...] + acc
    o_ref[...] = x
    ms = jnp.mean(x * x, axis=-1, keepdims=True)
    hn_ref[...] = (x * lax.rsqrt(ms + EPS) * g_ref[...]).astype(hn_ref.dtype)


def _outproj(att, mam, s5o, w_all, layer, res, g):
    m = att.shape[0]
    tm = min(m, 512)
    row = lambda width: pl.BlockSpec((tm, width), lambda i: (i, 0))
    return pl.pallas_call(
        _outproj_kernel,
        grid=(m // tm,),
        in_specs=[row(ATT_WIDTH), row(M_WIDTH), row(S5_WIDTH),
                  pl.BlockSpec((None, D_MODEL, D_MODEL), lambda i: (layer, 0, 0)),
                  row(D_MODEL), pl.BlockSpec((1, D_MODEL), lambda i: (0, 0))],
        out_specs=[row(D_MODEL), row(D_MODEL)],
        out_shape=[jax.ShapeDtypeStruct((m, D_MODEL), F32), jax.ShapeDtypeStruct((m, D_MODEL), BF16)],
        compiler_params=_cparams(("parallel",)),
        name="outproj",
    )(att, mam, s5o, w_all, res, g.reshape(1, D_MODEL).astype(F32))


def _ffn_conv(x, prev, w_ref, b_ref):
    y = b_ref[...] + x * w_ref[FFN_CONV - 1:FFN_CONV, :]
    for d in range(1, FFN_CONV):
        y = y + _shift_rows(x, prev, d) * w_ref[FFN_CONV - 1 - d:FFN_CONV - d, :]
    return y


def _ffn_fused_kernel(h_ref, wg_ref, wv_ref, pg_ref, pv_ref, cwg_ref, cwv_ref, cbg_ref, cbv_ref,
                      act_ref, fg_ref, fv_ref, cg_sc, cv_sc, *, tiles_per_seq):
    first = (pl.program_id(1) % tiles_per_seq) == 0
    h = h_ref[...]
    tm = h.shape[0]

    def branch(w_ref, prev_ref, carry_sc, cw_ref, cb_ref, f_ref):
        x = jnp.dot(h, w_ref[...], preferred_element_type=F32)
        prev = jnp.where(first, prev_ref[...], carry_sc[...])
        y = _ffn_conv(x, prev, cw_ref, cb_ref)
        tail = x[tm - (FFN_CONV - 1):tm, :]
        carry_sc[...] = tail
        f_ref[...] = tail
        return y

    gate = branch(wg_ref, pg_ref, cg_sc, cwg_ref, cbg_ref, fg_ref)
    val = branch(wv_ref, pv_ref, cv_sc, cwv_ref, cbv_ref, fv_ref)
    act_ref[...] = (gate * _sigmoid(gate) * val).astype(act_ref.dtype)


def _ffn_fused(h2, w_up_all, layer, prev, conv_w, conv_b, b, tp):
    m, d = h2.shape
    tm = min(tp, FFN_TM)
    tps = tp // tm
    tn = 512
    nj = D_FF // tn
    nc = FFN_CONV - 1
    col = lambda o: (lambda j, i: (0, j + o))
    wcol = lambda o: (lambda j, i: (layer, 0, j + o))
    prv = lambda o: pl.BlockSpec((None, nc, tn), lambda j, i: (i // tps, 0, j + o))
    kern = functools.partial(_ffn_fused_kernel, tiles_per_seq=tps)
    act, fg, fv = pl.pallas_call(
        kern,
        grid=(nj, m // tm),
        in_specs=[pl.BlockSpec((tm, d), lambda j, i: (i, 0)),
                  pl.BlockSpec((None, d, tn), wcol(0)), pl.BlockSpec((None, d, tn), wcol(nj)),
                  prv(0), prv(nj),
                  pl.BlockSpec((FFN_CONV, tn), col(0)), pl.BlockSpec((FFN_CONV, tn), col(nj)),
                  pl.BlockSpec((1, tn), col(0)), pl.BlockSpec((1, tn), col(nj))],
        out_specs=[pl.BlockSpec((tm, tn), lambda j, i: (i, j)),
                   pl.BlockSpec((None, nc, tn), lambda j, i: (i // tps, 0, j)),
                   pl.BlockSpec((None, nc, tn), lambda j, i: (i // tps, 0, j))],
        out_shape=[jax.ShapeDtypeStruct((m, D_FF), BF16),
                   jax.ShapeDtypeStruct((b, nc, D_FF), F32),
                   jax.ShapeDtypeStruct((b, nc, D_FF), F32)],
        scratch_shapes=[pltpu.VMEM((nc, tn), F32), pltpu.VMEM((nc, tn), F32)],
        compiler_params=_cparams(("parallel", "arbitrary")),
        name="ffn_up_act",
    )(h2, w_up_all, w_up_all, prev, prev, conv_w, conv_w, conv_b, conv_b)
    return act, jnp.concatenate([fg, fv], axis=-1)


def _matmul_kernel(a_ref, w_ref, o_ref):
    o_ref[...] = jnp.dot(a_ref[...], w_ref[...], preferred_element_type=F32).astype(o_ref.dtype)


def _ffn_up(a, w_all, layer, tn=512):
    m, k = a.shape
    n = w_all.shape[2]
    tm = min(m, 1024)
    return pl.pallas_call(
        _matmul_kernel,
        grid=(m // tm, n // tn),
        in_specs=[pl.BlockSpec((tm, k), lambda i, j: (i, 0)),
                  pl.BlockSpec((None, k, tn), lambda i, j: (layer, 0, j))],
        out_specs=pl.BlockSpec((tm, tn), lambda i, j: (i, j)),
        out_shape=jax.ShapeDtypeStruct((m, n), F32),
        compiler_params=_cparams(("parallel", "arbitrary")),
        name="ffn_up",
    )(a, w_all)


def _ffn_act_kernel(g_ref, v_ref, gp_ref, vp_ref, gw_ref, vw_ref, gb_ref, vb_ref, o_ref):
    gate = _ffn_conv(g_ref[...], gp_ref[...], gw_ref, gb_ref)
    val = _ffn_conv(v_ref[...], vp_ref[...], vw_ref, vb_ref)
    o_ref[...] = (gate * _sigmoid(gate) * val).astype(o_ref.dtype)


def _ffn_act_short(up3, prev, conv_w, conv_b):
    b, tp, _ = up3.shape
    tc = 512
    nj = D_FF // tc
    nc = FFN_CONV - 1
    main = lambda o: pl.BlockSpec((None, tp, tc), lambda bb, j: (bb, 0, j + o))
    prv = lambda o: pl.BlockSpec((None, nc, tc), lambda bb, j: (bb, 0, j + o))
    wsp = lambda o: pl.BlockSpec((FFN_CONV, tc), lambda bb, j: (0, j + o))
    bsp = lambda o: pl.BlockSpec((1, tc), lambda bb, j: (0, j + o))
    return pl.pallas_call(
        _ffn_act_kernel,
        grid=(b, nj),
        in_specs=[main(0), main(nj), prv(0), prv(nj), wsp(0), wsp(nj), bsp(0), bsp(nj)],
        out_specs=pl.BlockSpec((None, tp, tc), lambda bb, j: (bb, 0, j)),
        out_shape=jax.ShapeDtypeStruct((b, tp, D_FF), BF16),
        compiler_params=_cparams(("parallel", "parallel")),
        name="ffn_act",
    )(up3, up3, prev, prev, conv_w, conv_w, conv_b, conv_b)


def _down_kernel(a_ref, w_ref, res_ref, o_ref):
    o_ref[...] = res_ref[...] + jnp.dot(a_ref[...], w_ref[...], preferred_element_type=F32)


def _ffn_down(act, w_all, layer, res):
    m, k = act.shape
    tm = min(m, 1024)
    tn = 256
    return pl.pallas_call(
        _down_kernel,
        grid=(m // tm, D_MODEL // tn),
        in_specs=[pl.BlockSpec((tm, k), lambda i, j: (i, 0)),
                  pl.BlockSpec((None, k, tn), lambda i, j: (layer, 0, j)),
                  pl.BlockSpec((tm, tn), lambda i, j: (i, j))],
        out_specs=pl.BlockSpec((tm, tn), lambda i, j: (i, j)),
        out_shape=jax.ShapeDtypeStruct((m, D_MODEL), F32),
        compiler_params=_cparams(("parallel", "arbitrary")),
        name="ffn_down",
    )(act, w_all, res)


def _prep_layer_params(l, w_in, m_conv_w, m_conv_b, m_dt_bias, m_a_log, m_d, m_norm_g,
                       s5_lam_re, s5_lam_im, s5_log_step, s5_b_re, s5_b_im, s5_c_re, s5_c_im,
                       s5_d, s5_glu_w, s5_glu_b, w_out, ffn_w_up, ffn_conv_w, ffn_conv_b, ffn_w_down):
    def lanes(v):
        return jnp.zeros((LANES,), F32).at[SM_DT:SM_DT + M_HEADS].set(v.astype(F32))

    return {
        "w_in": _prep_w_in(w_in[l]),
        "m_conv_w": m_conv_w[l].astype(F32),
        "m_conv_b": m_conv_b[l].reshape(1, M_CONV_CH).astype(F32),
        "m_hp": jnp.stack([lanes(m_dt_bias[l]), lanes(m_a_log[l])], axis=0),
        "m_dexp": jnp.repeat(m_d[l].astype(F32), M_HEAD_DIM).reshape(1, M_WIDTH),
        "m_ng": m_norm_g[l].reshape(1, M_WIDTH).astype(F32),
        "s5_lam": jnp.stack([s5_lam_re[l].reshape(S5_N), s5_lam_im[l].reshape(S5_N),
                             jnp.repeat(s5_log_step[l], S5_STATE)], axis=0).astype(F32),
        "s5_bre": _block_diag_in(s5_b_re[l]).astype(BF16),
        "s5_bim": _block_diag_in(s5_b_im[l]).astype(BF16),
        "s5_cre": _block_diag_out(s5_c_re[l]).astype(BF16),
        "s5_cim": _block_diag_out(s5_c_im[l]).astype(BF16),
        "s5_dg": jnp.stack([s5_d[l], s5_glu_b[l]], axis=0).astype(F32),
        "s5_gw": s5_glu_w[l].astype(BF16),
        "ffn_conv_w": ffn_conv_w[l].astype(F32),
        "ffn_conv_b": ffn_conv_b[l].reshape(1, 2 * D_FF).astype(F32),
    }


def _layer(x3, rt, attend, mconv_prev, ssm_h0, s5_h0_re, s5_h0_im, fconv_prev, norm1_g, norm2_g, prm, wts,
           layer, depth, kv_prev, n_valid):
    b, tp, d = x3.shape
    m = b * tp
    x2 = x3.reshape(m, d)
    w_out_all, w_up_all, w_down_all = wts
    h = _rmsnorm(x2, norm1_g, BF16)
    p2, qkv2, k_all, v_all = _inproj(h, prm["w_in"], rt, layer, depth, kv_prev)
    p3 = p2.reshape(b, tp, N_P)
    att = attend(p3, qkv2.reshape(b, tp, N_QKV))
    mam, ssm_new = _mamba(p3, mconv_prev, ssm_h0, prm, n_valid)
    s5o, s5_re, s5_im = _s5(p3, s5_h0_re, s5_h0_im, prm, n_valid)
    x2, h2 = _outproj(att.reshape(m, ATT_WIDTH), mam.reshape(m, M_WIDTH), s5o.reshape(m, S5_WIDTH),
                      w_out_all, layer, x2, norm2_g)
    fconv_prev = fconv_prev.astype(F32)
    if tp >= SSD_L:
        act, fconv_new = _ffn_fused(h2, w_up_all, layer, fconv_prev, prm["ffn_conv_w"], prm["ffn_conv_b"], b, tp)
    else:
        up3 = _ffn_up(h2, w_up_all, layer).reshape(b, tp, 2 * D_FF)
        act = _ffn_act_short(up3, fconv_prev, prm["ffn_conv_w"], prm["ffn_conv_b"]).reshape(m, D_FF)
        fconv_new = jnp.concatenate([fconv_prev, up3[:, :n_valid]], axis=1)[:, n_valid:]
    x2 = _ffn_down(act, w_down_all, layer, x2)

    ik = p3[:, :n_valid, C_IK:C_IK + IDX_DIM]
    xbc_raw = jnp.concatenate([mconv_prev.astype(F32), p3[:, :n_valid, C_XBC:C_XBC + M_CONV_CH]], axis=1)
    mconv_new = xbc_raw[:, n_valid:]
    return x2.reshape(b, tp, d), (k_all, v_all), (ik, ssm_new, mconv_new, s5_re, s5_im, fconv_new)


def kernel(x_prompt, x_sample, cache_k, cache_v, cache_idx_k, state_ssm, state_mconv, state_s5_re, state_s5_im,
           state_fconv, page_table, norm1_g, w_in, m_conv_w, m_conv_b, m_dt_bias, m_a_log, m_d, m_norm_g,
           s5_lam_re, s5_lam_im, s5_log_step, s5_b_re, s5_b_im, s5_c_re, s5_c_im, s5_d, s5_glu_w, s5_glu_b,
           w_out, norm2_g, ffn_w_up, ffn_conv_w, ffn_conv_b, ffn_w_down, final_norm_g):
    bp, seq, d = x_prompt.shape
    bs, t_new, _ = x_sample.shape
    depth = w_in.shape[0]
    npages = page_table.shape[1]
    past = npages * PAGE_SIZE
    tpad = SUBLANES
    assert t_new <= tpad and seq % SSD_L == 0

    rt_p = _rope_tables(jnp.arange(seq, dtype=I32), bp)
    rt_s = _rope_tables(past + jnp.arange(tpad, dtype=I32), bs)
    hp = x_prompt.astype(F32)
    hs = jnp.pad(x_sample.astype(F32), ((0, 0), (0, tpad - t_new), (0, 0)))
    n_pool = cache_k.shape[1]
    pool_k = cache_k.reshape(depth, n_pool, PAGE_SIZE * ATT_HEADS, HEAD_DIM)
    pool_v = cache_v.reshape(depth, n_pool, PAGE_SIZE * ATT_HEADS, HEAD_DIM)
    topk_p = min(TOPK_MAX, seq // 4)
    topk_s = min(TOPK_MAX, (past + t_new) // 4)

    zero_mconv = jnp.zeros((bp, M_CONV - 1, M_CONV_CH), F32)
    zero_ssm = jnp.zeros((bp, M_HEADS, M_HEAD_DIM, M_STATE), F32)
    zero_s5 = jnp.zeros((bp, S5_GROUPS, S5_STATE), F32)
    zero_fconv = jnp.zeros((bp, FFN_CONV - 1, 2 * D_FF), F32)

    pool_ik_t = cache_idx_k.transpose(0, 1, 3, 2)
    wts = (w_out.astype(BF16), ffn_w_up.astype(BF16), ffn_w_down.astype(BF16))

    def attend_prompt(p3, qkv3):
        return _prompt_attention(qkv3, _prompt_index_bias(p3, topk_p))

    new_p, new_s = [], []
    kv_p = kv_s = None
    for l in range(depth):
        prm = _prep_layer_params(l, w_in, m_conv_w, m_conv_b, m_dt_bias, m_a_log, m_d, m_norm_g,
                                 s5_lam_re, s5_lam_im, s5_log_step, s5_b_re, s5_b_im, s5_c_re, s5_c_im,
                                 s5_d, s5_glu_w, s5_glu_b, w_out, ffn_w_up, ffn_conv_w, ffn_conv_b, ffn_w_down)

        def attend_sample(p3, qkv3, l=l):
            iq = p3[:, :, C_IQ:C_IQ + IDX_HEADS * IDX_DIM].reshape(bs, tpad, IDX_HEADS, IDX_DIM)
            iq_rows = iq.transpose(0, 2, 1, 3).reshape(bs, IDX_HEADS * tpad, IDX_DIM).astype(BF16)
            iw = p3[:, :, C_SM + SM_IW:C_SM + SM_IW + IDX_HEADS] * ((IDX_HEADS ** -0.5) * (IDX_DIM ** -0.5))
            w_col = iw.transpose(0, 2, 1).reshape(bs, IDX_HEADS * tpad, 1)
            pad_rows = ((0, 0), (0, LANES - tpad), (0, 0))
            ik_new_t = jnp.pad(p3[:, :, C_IK:C_IK + IDX_DIM], pad_rows).transpose(0, 2, 1)
            mask = _sample_index_mask(page_table, iq_rows, w_col, ik_new_t, pool_ik_t, l, t_new, topk_s)
            q = qkv3[:, :, C_Q:C_Q + ATT_WIDTH].reshape(bs, tpad, ATT_HEADS, HEAD_DIM).transpose(0, 2, 1, 3)
            eye = jnp.eye(ATT_HEADS, dtype=BF16)
            qbd = (q[:, :, :, None, :] * eye[None, :, None, :, None]).reshape(bs, ATT_HEADS * tpad, ATT_WIDTH)
            k_new = jnp.pad(qkv3[:, :, C_K:C_K + ATT_WIDTH], pad_rows)
            v_new = jnp.pad(qkv3[:, :, C_V:C_V + ATT_WIDTH], pad_rows)
            return _sample_attention(page_table, qbd, mask, k_new, v_new, pool_k, pool_v, l)

        hp, kv_p, st_p = _layer(hp, rt_p, attend_prompt, zero_mconv, zero_ssm, zero_s5, zero_s5, zero_fconv,
                                norm1_g[l], norm2_g[l], prm, wts, l, depth, kv_p, seq)
        hs, kv_s, st_s = _layer(hs, rt_s, attend_sample, state_mconv[l], state_ssm[l], state_s5_re[l],
                                state_s5_im[l], state_fconv[l], norm1_g[l], norm2_g[l], prm, wts, l, depth,
                                kv_s, t_new)
        new_p.append(st_p)
        new_s.append(st_s)

    def stacked(states, i):
        return jnp.stack([s[i] for s in states], axis=0)

    def heads(kv, b, tp, n_valid):
        return kv.reshape(depth, b, tp, ATT_HEADS, HEAD_DIM)[:, :, :n_valid]

    y_prompt = _rmsnorm(hp.reshape(bp * seq, d), final_norm_g, F32).reshape(bp, seq, d)
    y_sample = _rmsnorm(hs.reshape(bs * tpad, d), final_norm_g, F32).reshape(bs, tpad, d)[:, :t_new]
    return (y_prompt, y_sample,
            heads(kv_p[0], bp, seq, seq), heads(kv_p[1], bp, seq, seq),
            stacked(new_p, 0), stacked(new_p, 1), stacked(new_p, 2), stacked(new_p, 3),
            stacked(new_p, 4), stacked(new_p, 5),
            heads(kv_s[0], bs, tpad, t_new), heads(kv_s[1], bs, tpad, t_new),
            stacked(new_s, 0), stacked(new_s, 1), stacked(new_s, 2), stacked(new_s, 3),
            stacked(new_s, 4), stacked(new_s, 5))
```

```python
import functools
import math

import numpy as np
import jax
import jax.numpy as jnp
from jax import lax
from jax.experimental import pallas as pl
from jax.experimental.pallas import tpu as pltpu

F32 = jnp.float32
BF16 = jnp.bfloat16
I32 = jnp.int32

D_MODEL = 2048
HEAD_DIM = 128
ATT_WIDTH = D_MODEL // 2
ATT_HEADS = ATT_WIDTH // HEAD_DIM
IDX_HEADS = 16
IDX_DIM = 64
TOPK_MAX = 256
ROPE_THETA = 10000.0
PAGE_SIZE = 128
M_WIDTH = D_MODEL // 4
M_HEAD_DIM = 64
M_HEADS = M_WIDTH // M_HEAD_DIM
M_GROUPS = 2
M_STATE = 128
M_CONV = 4
M_CONV_CH = M_WIDTH + 2 * M_GROUPS * M_STATE
S5_WIDTH = D_MODEL - ATT_WIDTH - M_WIDTH
S5_GROUP = 16
S5_GROUPS = S5_WIDTH // S5_GROUP
S5_STATE = 64
S5_N = S5_GROUPS * S5_STATE
D_FF = (D_MODEL * 11) // 4
FFN_CONV = 3
EPS = 1e-6
IN_SIZES = (ATT_WIDTH, ATT_WIDTH, ATT_WIDTH, IDX_HEADS * IDX_DIM, IDX_DIM, IDX_HEADS,
            M_WIDTH, M_CONV_CH, M_HEADS, S5_WIDTH)

LANES = 128
SUBLANES = 8
VMEM_LIMIT = 48 * 1024 * 1024
INPROJ_VMEM_LIMIT = 56 * 1024 * 1024

N_QKV = 3 * ATT_WIDTH
C_Q, C_K, C_V = 0, 1024, 2048
C_IQ, C_XBC, C_Z, C_U, C_IK, C_SM = 0, 1024, 2048, 2560, 3072, 3200
N_P = 3584
PROJ_TN = 512
PROJ_TM = 1024
NJ_QKV = N_QKV // PROJ_TN
SM_IW = 0
SM_DT = 16
SSD_L = 128
FFN_TM = 1024
COUNT_ACC = 4
ATT_ROWS = 128
PROJ_RC = 256
ATT_PAGES = 16
S5_KB = 256
NEG_BIG = -1e30
INT_MIN = -2 ** 31
F32_MIN_NORMAL = 2.0 ** -126
BF16_MIN_NORMAL_BITS = 0x0080
LOG2E = math.log2(math.e)
Q_SCALE = (HEAD_DIM ** -0.5) * LOG2E
ATT_KT = 256


def _cparams(sem):
    return pltpu.CompilerParams(dimension_semantics=sem, vmem_limit_bytes=VMEM_LIMIT)


def _iota(shape, dim):
    return lax.broadcasted_iota(I32, shape, dim)


def _sigmoid(x):
    return 1.0 / (1.0 + jnp.exp(-x))


def _nt_dot(a, b):
    return lax.dot_general(a, b, (((1,), (1,)), ((), ())), preferred_element_type=F32)


def _rmsnorm_kernel(x_ref, g_ref, o_ref):
    x = x_ref[...]
    ms = jnp.mean(x * x, axis=-1, keepdims=True)
    o_ref[...] = (x * lax.rsqrt(ms + EPS) * g_ref[...]).astype(o_ref.dtype)


def _rmsnorm(x2d, g, out_dtype):
    m, d = x2d.shape
    tm = min(m, 512)
    return pl.pallas_call(
        _rmsnorm_kernel,
        grid=(m // tm,),
        in_specs=[pl.BlockSpec((tm, d), lambda i: (i, 0)), pl.BlockSpec((1, d), lambda i: (0, 0))],
        out_specs=pl.BlockSpec((tm, d), lambda i: (i, 0)),
        out_shape=jax.ShapeDtypeStruct((m, d), out_dtype),
        compiler_params=_cparams(("parallel",)),
        name="rmsnorm",
    )(x2d, g.reshape(1, d).astype(F32))


def _rope_tables(pos):
    half, half2 = HEAD_DIM // 2, IDX_DIM // 2
    inv = ROPE_THETA ** (-jnp.arange(half, dtype=F32) / half)
    inv2 = ROPE_THETA ** (-jnp.arange(half2, dtype=F32) / half2)
    inv_col = jnp.concatenate([jnp.tile(inv, 4), jnp.tile(inv2, 8)])
    col = np.arange(4 * LANES)
    is_cos = (col // LANES) % 2 == 0
    first_half = np.where(col < 2 * LANES, (col % HEAD_DIM) < half, (col % IDX_DIM) < half2)
    sign = np.where(first_half, -1.0, 1.0).astype(np.float32)
    ang = pos.astype(F32)[:, None] * inv_col[None, :]
    return jnp.where(jnp.asarray(is_cos)[None, :], jnp.cos(ang), jnp.asarray(sign)[None, :] * jnp.sin(ang))


def _rope128(x, c, s):
    return x * c + pltpu.roll(x, 64, 1) * s


def _rope64(x, c, s):
    lane = _iota(x.shape, 1)
    part = jnp.where((lane & 32) == 0, pltpu.roll(x, 96, 1), pltpu.roll(x, 32, 1))
    return x * c + part * s


def _inproj_kernel(h_ref, w_ref, rt_ref, *rest):
    p_ref, qkv_ref, k4_ref, v4_ref = rest[-4:]
    j = pl.program_id(1)
    nb = PROJ_TN // LANES
    tm = h_ref.shape[0]
    rc = min(tm, PROJ_RC)

    def run(epilogue):
        for r0 in range(0, tm, rc):
            rs = slice(r0, r0 + rc)
            epilogue(r0, rs, _nt_dot(h_ref[rs, :], w_ref[...]))

    def head_major(dst_ref, r0, x, hb0):
        for hb in range(nb):
            dst_ref[pl.ds(r0 * ATT_HEADS + hb0 + hb, rc, stride=ATT_HEADS), :] = x[:, hb * 128:(hb + 1) * 128]

    def rope_qk(rs, acc):
        c, s = rt_ref[rs, 0:128], rt_ref[rs, 128:256]
        return jnp.concatenate([_rope128(acc[:, hb * 128:(hb + 1) * 128], c, s) for hb in range(nb)], axis=1)

    @pl.when(j < 2)
    def _():
        def q_epilogue(r0, rs, acc):
            qkv_ref[rs, :] = (rope_qk(rs, acc) * Q_SCALE).astype(BF16)
        run(q_epilogue)

    for half in range(2):
        @pl.when(j == 2 + half)
        def _():
            def k_epilogue(r0, rs, acc):
                r = rope_qk(rs, acc)
                qkv_ref[rs, :] = r.astype(BF16)
                head_major(k4_ref, r0, r, half * nb)
            run(k_epilogue)

        @pl.when(j == 4 + half)
        def _():
            def v_epilogue(r0, rs, acc):
                qkv_ref[rs, :] = acc.astype(BF16)
                head_major(v4_ref, r0, acc, half * nb)
            run(v_epilogue)

    @pl.when((j >= 6) & (j < 8))
    def _():
        def iq_epilogue(r0, rs, acc):
            c, s = rt_ref[rs, 256:384], rt_ref[rs, 384:512]
            for hb in range(nb):
                p_ref[rs, hb * 128:(hb + 1) * 128] = _rope64(acc[:, hb * 128:(hb + 1) * 128], c, s)
        run(iq_epilogue)

    @pl.when(j == 12)
    def _():
        def tail_epilogue(r0, rs, acc):
            c, s = rt_ref[rs, 256:384], rt_ref[rs, 384:512]
            p_ref[rs, 0:128] = _rope64(acc[:, 0:128], c, s)
            p_ref[rs, 128:] = acc[:, 128:]
        run(tail_epilogue)

    @pl.when((j >= 8) & (j < 12))
    def _():
        def plain_epilogue(r0, rs, acc):
            p_ref[rs, :] = acc
        run(plain_epilogue)


def _inproj(h, w, rt, layer, depth, kv_prev):
    m, d = h.shape
    tm = min(m, PROJ_TM)
    nj = w.shape[0] // PROJ_TN
    n_rt = rt.shape[0] // tm
    kv_spec = pl.BlockSpec((None, tm * ATT_HEADS, HEAD_DIM), lambda i, j: (layer, i, 0))
    kv_shape = jax.ShapeDtypeStruct((depth, m * ATT_HEADS, HEAD_DIM), F32)
    in_specs = [pl.BlockSpec((tm, d), lambda i, j: (i, 0)),
                pl.BlockSpec((PROJ_TN, d), lambda i, j: (j, 0)),
                pl.BlockSpec((tm, 512), lambda i, j: (i % n_rt, 0))]
    args = [h, w, rt]
    aliases = {}
    if kv_prev is not None:
        in_specs += [pl.BlockSpec(memory_space=pl.ANY)] * 2
        args += list(kv_prev)
        aliases = {3: 2, 4: 3}
    return pl.pallas_call(
        _inproj_kernel,
        grid=(m // tm, nj),
        in_specs=in_specs,
        out_specs=[pl.BlockSpec((tm, PROJ_TN), lambda i, j: (i, jnp.maximum(j - NJ_QKV, 0))),
                   pl.BlockSpec((tm, PROJ_TN), lambda i, j: (i, jnp.minimum(j, NJ_QKV - 1))),
                   kv_spec, kv_spec],
        out_shape=[jax.ShapeDtypeStruct((m, N_P), F32), jax.ShapeDtypeStruct((m, N_QKV), BF16),
                   kv_shape, kv_shape],
        input_output_aliases=aliases,
        compiler_params=pltpu.CompilerParams(dimension_semantics=("parallel", "arbitrary"),
                                             vmem_limit_bytes=INPROJ_VMEM_LIMIT),
        name="inproj",
    )(*args)


def _prep_w_in(w):
    wt = w.T
    parts, start = [], 0
    for sz in IN_SIZES:
        parts.append(wt[start:start + sz])
        start += sz
    q, k, v, iq, ik, iw, z, xbc, dt, u = parts
    d = w.shape[0]
    small = jnp.concatenate([iw, dt, jnp.zeros((LANES - IDX_HEADS - M_HEADS, d), w.dtype)], axis=0)
    tail = jnp.zeros((N_P - C_SM - LANES, d), w.dtype)
    return jnp.concatenate([q, k, v, iq, xbc, z, u, ik, ik, small, tail], axis=0).astype(BF16)


def _sort_key(x):
    bits = pltpu.bitcast(x, I32)
    return jnp.where(bits < 0, bits ^ jnp.int32(0x7FFFFFFF), bits)


def _radix_select(count_fn, k_eff, shape, idx_bits, n_total, count_hi_fn=None):
    kf = k_eff.astype(F32)
    start = (jnp.full(shape, INT_MIN, I32), jnp.zeros(shape, F32) + n_total)
    low_bits = 32

    if count_hi_fn is not None:
        def hi_body(it, carry):
            c, cnt_c = carry
            cand = c + (jnp.int32(1) << (15 - it))
            bits = jnp.where(cand < 0, cand ^ jnp.int32(0x7FFF), cand) & jnp.int32(0xFFFF)
            bits = jnp.where((bits > 0) & (bits < BF16_MIN_NORMAL_BITS), BF16_MIN_NORMAL_BITS, bits)
            value = pltpu.bitcast(bits << 16, F32).astype(BF16)
            cnt = count_hi_fn(value)
            take = cnt >= kf
            return jnp.where(take, cand, c), jnp.where(take, cnt, cnt_c)

        c, cnt_c = lax.fori_loop(0, 16, hi_body, (jnp.full(shape, -(1 << 15), I32), start[1]))
        start = (c << 16, cnt_c)
        low_bits = 16

    def bit_body(it, carry):
        t, cnt_t = carry
        cand = t + (jnp.int32(1) << (low_bits - 1 - it))
        cnt = count_fn(lambda key, idx: jnp.where(key >= cand, 1.0, 0.0))
        take = cnt >= kf
        return jnp.where(take, cand, t), jnp.where(take, cnt, cnt_t)

    t, cnt_t = lax.fori_loop(0, low_bits, bit_body, start)

    def with_ties():
        need = kf - count_fn(lambda key, idx: jnp.where(key > t, 1.0, 0.0))

        def tie_body(it, c):
            cand = c + (jnp.int32(1) << (idx_bits - 1 - it))
            cnt = count_fn(lambda key, idx: jnp.where(key == t, jnp.where(idx < cand, 1.0, 0.0), 0.0))
            return jnp.where(cnt < need, cand, c)

        return lax.fori_loop(0, idx_bits, tie_body, jnp.zeros(shape, I32))

    c = lax.cond(jnp.max(cnt_t - kf) > 0.0, with_ties, lambda: jnp.full(shape, (1 << idx_bits) - 1, I32))
    return t, c


def _selected(key, idx, t, c):
    return jnp.where(key > t, 1.0, jnp.where(key == t, jnp.where(idx <= c, 1.0, 0.0), 0.0))


def _pidx_kernel(iq_ref, ik_ref, wt_ref, bias_ref, key_sc, hi_sc, iqb_sc, *, tq, tk, topk, seq):
    i = pl.program_id(1)
    n_chunks = ((i + 1) * tq + tk - 1) // tk
    iqb_sc[...] = iq_ref[...].astype(BF16)
    wt = wt_ref[...] * ((IDX_HEADS ** -0.5) * (IDX_DIM ** -0.5))
    qpos = i * tq + _iota((1, tq), 1)
    lane_k = _iota((tk, LANES), 1)
    bias_ref[...] = jnp.full(bias_ref.shape, NEG_BIG, bias_ref.dtype)

    def score_chunk(c, carry):
        off = pl.multiple_of(c * tk, tk)
        ik = ik_ref[pl.ds(off, tk), :]
        ik_e = jnp.where(lane_k < IDX_DIM, ik, 0.0).astype(BF16)
        ik_o = jnp.where(lane_k >= IDX_DIM, ik, 0.0).astype(BF16)
        acc = jnp.zeros((tk, tq), F32)
        for p in range(IDX_HEADS // 2):
            qp = iqb_sc[:, p * 128:(p + 1) * 128]
            acc = acc + wt[2 * p:2 * p + 1, :] * jnp.maximum(_nt_dot(ik_e, qp), 0.0)
            acc = acc + wt[2 * p + 1:2 * p + 2, :] * jnp.maximum(_nt_dot(ik_o, qp), 0.0)
        kpos = off + _iota((tk, tq), 0)
        acc = jnp.where(kpos <= qpos, acc, -jnp.inf)
        acc = jnp.where(jnp.abs(acc) < F32_MIN_NORMAL, 0.0, acc)
        key_sc[pl.ds(off, tk), :] = _sort_key(acc)
        trunc = pltpu.bitcast(acc, I32) & jnp.int32(-65536)
        hi_sc[pl.ds(off, tk), :] = pltpu.bitcast(trunc, F32).astype(BF16)
        return carry

    lax.fori_loop(0, n_chunks, score_chunk, 0)
    acc_rows = COUNT_ACC * SUBLANES

    def count_hi_fn(value):
        def body(c, part):
            off = pl.multiple_of(c * tk, tk)
            m = jnp.where(hi_sc[pl.ds(off, tk), :] >= value, jnp.ones((), BF16), jnp.zeros((), BF16))
            m = m.reshape(tk // acc_rows, acc_rows, tq)
            s = m[0]
            for r in range(1, tk // acc_rows):
                s = s + m[r]
            return part + s.astype(F32)
        part = lax.fori_loop(0, n_chunks, body, jnp.zeros((acc_rows, tq), F32))
        return jnp.sum(part, axis=0, keepdims=True)

    def count_fn(pred):
        def body(c, part):
            off = pl.multiple_of(c * tk, tk)
            key = key_sc[pl.ds(off, tk), :]
            idx = off + _iota((tk, tq), 0)
            m = pred(key, idx)
            return part + jnp.sum(m.reshape(tk // acc_rows, acc_rows, tq), axis=0)
        part = lax.fori_loop(0, n_chunks, body, jnp.zeros((acc_rows, tq), F32))
        return jnp.sum(part, axis=0, keepdims=True)

    k_eff = jnp.minimum(topk, qpos + 1)
    t, c = _radix_select(count_fn, k_eff, (1, tq), max(1, (seq - 1).bit_length()),
                         (n_chunks * tk).astype(F32), count_hi_fn)

    def write_chunk(cc, carry):
        off = pl.multiple_of(cc * tk, tk)
        key = key_sc[pl.ds(off, tk), :]
        idx = off + _iota((tk, tq), 0)
        bias_t = (_selected(key, idx, t, c) - 1.0) * (-NEG_BIG)
        bias_ref[:, pl.ds(off, tk)] = bias_t.T.astype(bias_ref.dtype)
        return carry

    lax.fori_loop(0, n_chunks, write_chunk, 0)


def _prompt_index_bias(p3, topk):
    b, s, _ = p3.shape
    tq = min(s, 256)
    tk = min(s, 512)
    wt = p3[:, :, C_SM + SM_IW:C_SM + SM_IW + IDX_HEADS].transpose(0, 2, 1)
    kern = functools.partial(_pidx_kernel, tq=tq, tk=tk, topk=topk, seq=s)
    return pl.pallas_call(
        kern,
        grid=(b, s // tq),
        in_specs=[pl.BlockSpec((None, tq, 1024), lambda bb, i: (bb, i, C_IQ // 1024)),
                  pl.BlockSpec((None, s, LANES), lambda bb, i: (bb, 0, C_IK // LANES)),
                  pl.BlockSpec((None, IDX_HEADS, tq), lambda bb, i: (bb, 0, i))],
        out_specs=pl.BlockSpec((None, tq, s), lambda bb, i: (bb, i, 0)),
        out_shape=jax.ShapeDtypeStruct((b, s, s), BF16),
        scratch_shapes=[pltpu.VMEM((s, tq), I32), pltpu.VMEM((s, tq), BF16), pltpu.VMEM((tq, 1024), BF16)],
        compiler_params=_cparams(("parallel", "arbitrary")),
        name="prompt_index_bias",
    )(p3, p3, wt)


def _pattn_kernel(qi_ref, ki_ref, q_ref, k_ref, v_ref, b_ref, o_ref, m_sc, l_sc, acc_sc, *, tq, tk):
    st = pl.program_id(1)
    i = qi_ref[st]
    kc = ki_ref[st]
    last = ((i + 1) * tq - 1) // tk
    kt = min(tk, ATT_KT)
    rep = kt // LANES

    @pl.when(kc == 0)
    def _():
        m_sc[...] = jnp.full(m_sc.shape, NEG_BIG, F32)
        l_sc[...] = jnp.zeros(l_sc.shape, F32)
        acc_sc[...] = jnp.zeros(acc_sc.shape, F32)

    ones = jnp.ones((kt, HEAD_DIM), BF16)
    tr = min(tq, ATT_ROWS)
    for r0 in range(0, tq, tr):
        rs = slice(r0, r0 + tr)
        for k0 in range(0, tk, kt):
            ks = slice(k0, k0 + kt)
            bias = b_ref[rs, ks].astype(F32)
            for h in range(ATT_HEADS):
                sl = slice(h * HEAD_DIM, (h + 1) * HEAD_DIM)
                s = _nt_dot(q_ref[rs, sl], k_ref[ks, sl]) + bias
                m_old = m_sc[rs, sl]
                m_new = jnp.maximum(m_old, jnp.max(s, axis=1, keepdims=True))
                alpha = jnp.exp2(m_old - m_new)
                p = jnp.exp2(s - jnp.concatenate([m_new] * rep, axis=1)).astype(BF16)
                pv = jnp.dot(p, jnp.concatenate([v_ref[ks, sl], ones], axis=1), preferred_element_type=F32)
                acc_sc[rs, sl] = alpha * acc_sc[rs, sl] + pv[:, 0:HEAD_DIM]
                l_sc[rs, sl] = alpha * l_sc[rs, sl] + pv[:, HEAD_DIM:]
                m_sc[rs, sl] = m_new

    @pl.when(kc == last)
    def _():
        o_ref[...] = (acc_sc[...] / l_sc[...]).astype(o_ref.dtype)


def _prompt_attention(qkv3, bias):
    b, s, _ = qkv3.shape
    tq = min(s, 256)
    tk = min(s, 512)
    steps = [(i, kc) for i in range(s // tq) for kc in range(((i + 1) * tq - 1) // tk + 1)]
    qi = jnp.asarray(np.array([a for a, _ in steps], np.int32))
    ki = jnp.asarray(np.array([c for _, c in steps], np.int32))
    kern = functools.partial(_pattn_kernel, tq=tq, tk=tk)
    grid_spec = pltpu.PrefetchScalarGridSpec(
        num_scalar_prefetch=2,
        grid=(b, len(steps)),
        in_specs=[pl.BlockSpec((None, tq, ATT_WIDTH), lambda bb, st, qi, ki: (bb, qi[st], C_Q // ATT_WIDTH)),
                  pl.BlockSpec((None, tk, ATT_WIDTH), lambda bb, st, qi, ki: (bb, ki[st], C_K // ATT_WIDTH)),
                  pl.BlockSpec((None, tk, ATT_WIDTH), lambda bb, st, qi, ki: (bb, ki[st], C_V // ATT_WIDTH)),
                  pl.BlockSpec((None, tq, tk), lambda bb, st, qi, ki: (bb, qi[st], ki[st]))],
        out_specs=pl.BlockSpec((None, tq, ATT_WIDTH), lambda bb, st, qi, ki: (bb, qi[st], 0)),
        scratch_shapes=[pltpu.VMEM((tq, ATT_WIDTH), F32)] * 3,
    )
    return pl.pallas_call(
        kern,
        grid_spec=grid_spec,
        out_shape=jax.ShapeDtypeStruct((b, s, ATT_WIDTH), BF16),
        compiler_params=_cparams(("parallel", "arbitrary")),
        name="prompt_attention",
    )(qi, ki, qkv3, qkv3, qkv3, bias)


def _sidx_kernel(pt_ref, iq_ref, w_ref, iknew_ref, *rest, pg, nsteps, past, n_new, topk):
    pages = rest[:pg]
    mask_ref = rest[pg]
    key_sc = rest[pg + 1]
    st = pl.program_id(1)
    iq = iq_ref[...]
    w = w_ref[...]
    width = past + LANES

    def score(ikp_t):
        d = jnp.maximum(jnp.dot(iq, ikp_t.astype(BF16), preferred_element_type=F32), 0.0) * w
        acc = d[0:SUBLANES]
        for h in range(1, IDX_HEADS):
            acc = acc + d[h * SUBLANES:(h + 1) * SUBLANES]
        return acc

    for r in range(pg):
        off = pl.multiple_of((st * pg + r) * PAGE_SIZE, PAGE_SIZE)
        key_sc[:, pl.ds(off, PAGE_SIZE)] = _sort_key(score(pages[r][...]))

    @pl.when(st == nsteps - 1)
    def _():
        trow = _iota((SUBLANES, LANES), 0)
        jcol = _iota((SUBLANES, LANES), 1)
        sc = score(iknew_ref[...])
        sc = jnp.where(jcol <= trow, jnp.where(jcol < n_new, sc, -jnp.inf), -jnp.inf)
        key_sc[:, past:width] = _sort_key(sc)
        key = key_sc[...]
        idx = _iota((SUBLANES, width), 1)

        def count_fn(pred):
            return jnp.sum(pred(key, idx), axis=1, keepdims=True)

        qpos = past + _iota((SUBLANES, 1), 0)
        k_eff = jnp.minimum(topk, qpos + 1)
        t, c = _radix_select(count_fn, k_eff, (SUBLANES, 1), max(1, (width - 1).bit_length()), float(width))
        mask_ref[...] = _selected(key, idx, t, c)


def _sample_index_mask(page_table, iq_rows, w_col, ik_new, pool_ik, layer, n_new, topk):
    b, npages = page_table.shape
    pg = 8 if npages % 8 == 0 else npages
    nsteps = npages // pg
    past = npages * PAGE_SIZE
    width = past + LANES

    def page_spec(r):
        return pl.BlockSpec((None, None, IDX_DIM, PAGE_SIZE),
                            lambda bb, st, pt: (layer, pt[bb, st * pg + r], 0, 0))

    kern = functools.partial(_sidx_kernel, pg=pg, nsteps=nsteps, past=past, n_new=n_new, topk=topk)
    grid_spec = pltpu.PrefetchScalarGridSpec(
        num_scalar_prefetch=1,
        grid=(b, nsteps),
        in_specs=[pl.BlockSpec((None, IDX_HEADS * SUBLANES, IDX_DIM), lambda bb, st, pt: (bb, 0, 0)),
                  pl.BlockSpec((None, IDX_HEADS * SUBLANES, 1), lambda bb, st, pt: (bb, 0, 0)),
                  pl.BlockSpec((None, IDX_DIM, LANES), lambda bb, st, pt: (bb, 0, 0))]
                 + [page_spec(r) for r in range(pg)],
        out_specs=pl.BlockSpec((None, SUBLANES, width), lambda bb, st, pt: (bb, 0, 0)),
        scratch_shapes=[pltpu.VMEM((SUBLANES, width), I32)],
    )
    return pl.pallas_call(
        kern,
        grid_spec=grid_spec,
        out_shape=jax.ShapeDtypeStruct((b, SUBLANES, width), F32),
        compiler_params=_cparams(("parallel", "arbitrary")),
        name="sample_index_mask",
    )(page_table, iq_rows, w_col, ik_new, *([pool_ik] * pg))


def _sattn_kernel(pt_ref, qbd_ref, mask_ref, knew_ref, vnew_ref, *rest, pg, nsteps, past):
    kpages = rest[:pg]
    vpages = rest[pg:2 * pg]
    o_ref = rest[2 * pg]
    m_sc, l_sc, acc_sc = rest[2 * pg + 1:]
    st = pl.program_id(1)

    @pl.when(st == 0)
    def _():
        m_sc[...] = jnp.full(m_sc.shape, NEG_BIG, F32)
        l_sc[...] = jnp.zeros(l_sc.shape, F32)
        acc_sc[...] = jnp.zeros(acc_sc.shape, F32)

    qbd = qbd_ref[...]

    def page2d(ref):
        return jnp.concatenate([ref[pl.ds(h, PAGE_SIZE, stride=ATT_HEADS), :] for h in range(ATT_HEADS)],
                               axis=1).astype(BF16)

    def update(k_list, v_list, mk8):
        n = len(k_list)
        mk = jnp.concatenate([mk8] * ATT_HEADS, axis=0) > 0
        s = jnp.concatenate([_nt_dot(qbd, k) for k in k_list], axis=1)
        s = jnp.where(mk, s, NEG_BIG)
        m_old = m_sc[...]
        m_new = jnp.maximum(m_old, jnp.max(s, axis=1, keepdims=True))
        alpha = jnp.exp2(m_old - m_new)
        p = jnp.where(mk, jnp.exp2(s - jnp.concatenate([m_new] * n, axis=1)), 0.0)
        l_sc[...] = alpha * l_sc[...] + jnp.sum(p, axis=1, keepdims=True)
        pb = p.astype(BF16)
        pv = jnp.dot(pb[:, 0:LANES], v_list[0], preferred_element_type=F32)
        for r in range(1, n):
            pv = pv + jnp.dot(pb[:, r * LANES:(r + 1) * LANES], v_list[r], preferred_element_type=F32)
        acc_sc[...] = jnp.concatenate([alpha] * ATT_HEADS, axis=1) * acc_sc[...] + pv
        m_sc[...] = m_new

    off = pl.multiple_of(st * pg * PAGE_SIZE, pg * PAGE_SIZE)
    update([page2d(r) for r in kpages], [page2d(r) for r in vpages], mask_ref[:, pl.ds(off, pg * PAGE_SIZE)])

    @pl.when(st == nsteps - 1)
    def _():
        update([knew_ref[...]], [vnew_ref[...]], mask_ref[:, past:past + LANES])
        for h in range(ATT_HEADS):
            rs = slice(h * SUBLANES, (h + 1) * SUBLANES)
            cs = slice(h * HEAD_DIM, (h + 1) * HEAD_DIM)
            o_ref[:, cs] = (acc_sc[rs, cs] / l_sc[rs, :]).astype(o_ref.dtype)


def _sample_attention(page_table, qbd, mask, k_new, v_new, pool_k, pool_v, layer):
    b, npages = page_table.shape
    pg = ATT_PAGES if npages % ATT_PAGES == 0 else npages
    nsteps = npages // pg
    past = npages * PAGE_SIZE
    width = past + LANES
    rows = ATT_HEADS * SUBLANES

    def page_spec(r):
        return pl.BlockSpec((None, None, PAGE_SIZE * ATT_HEADS, HEAD_DIM),
                            lambda bb, st, pt: (layer, pt[bb, st * pg + r], 0, 0))

    kern = functools.partial(_sattn_kernel, pg=pg, nsteps=nsteps, past=past)
    grid_spec = pltpu.PrefetchScalarGridSpec(
        num_scalar_prefetch=1,
        grid=(b, nsteps),
        in_specs=[pl.BlockSpec((None, rows, ATT_WIDTH), lambda bb, st, pt: (bb, 0, 0)),
                  pl.BlockSpec((None, SUBLANES, width), lambda bb, st, pt: (bb, 0, 0)),
                  pl.BlockSpec((None, LANES, ATT_WIDTH), lambda bb, st, pt: (bb, 0, 0)),
                  pl.BlockSpec((None, LANES, ATT_WIDTH), lambda bb, st, pt: (bb, 0, 0))]
                 + [page_spec(r) for r in range(pg)] * 2,
        out_specs=pl.BlockSpec((None, SUBLANES, ATT_WIDTH), lambda bb, st, pt: (bb, 0, 0)),
        scratch_shapes=[pltpu.VMEM((rows, LANES), F32), pltpu.VMEM((rows, LANES), F32),
                        pltpu.VMEM((rows, ATT_WIDTH), F32)],
    )
    return pl.pallas_call(
        kern,
        grid_spec=grid_spec,
        out_shape=jax.ShapeDtypeStruct((b, SUBLANES, ATT_WIDTH), BF16),
        compiler_params=_cparams(("parallel", "arbitrary")),
        name="sample_attention",
    )(page_table, qbd, mask, k_new, v_new, *([pool_k] * pg), *([pool_v] * pg))


def _shift_rows(x, prev, d):
    r = pltpu.roll(x, d, 0)
    rowi = _iota(x.shape, 0)
    np_ = prev.shape[0]
    for q in range(d):
        r = jnp.where(rowi == q, prev[np_ - d + q:np_ - d + q + 1, :], r)
    return r


def _per_head(v):
    lane = _iota((v.shape[0], LANES), 1)
    outs = []
    for j in range(M_HEADS // 2):
        a = v[:, SM_DT + 2 * j:SM_DT + 2 * j + 1]
        b = v[:, SM_DT + 2 * j + 1:SM_DT + 2 * j + 2]
        outs.append(jnp.where(lane < M_HEAD_DIM, a, b))
    return jnp.concatenate(outs, axis=1)


def _mamba_kernel(xbc_ref, halo_ref, z_ref, sm_ref, cprev_ref, h0_ref, cw_ref, cb_ref, hp_ref,
                  dexp_ref, ng_ref, y_ref, hout_ref, h_sc, *, rows, n_valid, nsteps):
    c = pl.program_id(1)
    L = SSD_L
    hi = lax.Precision.HIGHEST

    @pl.when(c == 0)
    def _():
        h_sc[...] = h0_ref[...]

    x = xbc_ref[...]
    prev = jnp.where(c == 0, cprev_ref[...], halo_ref[SUBLANES - (M_CONV - 1):SUBLANES, :])
    conv = cb_ref[...] + x * cw_ref[M_CONV - 1:M_CONV, :]
    for d in range(1, M_CONV):
        conv = conv + _shift_rows(x, prev, d) * cw_ref[M_CONV - 1 - d:M_CONV - d, :]
    xc = conv * _sigmoid(conv)

    smv = sm_ref[...] + hp_ref[0:1, :]
    dt = jnp.maximum(smv, 0.0) + jnp.log1p(jnp.exp(-jnp.abs(smv)))
    rowg = c * rows + _iota((rows, LANES), 0)
    dt = jnp.where(rowg < n_valid, dt, 0.0)
    zz = z_ref[...]
    if rows < L:
        xc = jnp.concatenate([xc, jnp.zeros((L - rows, xc.shape[1]), F32)], axis=0)
        dt = jnp.concatenate([dt, jnp.zeros((L - rows, LANES), F32)], axis=0)
        zz = jnp.concatenate([zz, jnp.zeros((L - rows, zz.shape[1]), F32)], axis=0)

    xs = xc[:, 0:M_WIDTH]
    bm = xc[:, M_WIDTH:M_WIDTH + M_GROUPS * M_STATE].astype(BF16)
    cm = xc[:, M_WIDTH + M_GROUPS * M_STATE:].astype(BF16)

    da = dt * (-jnp.exp(hp_ref[1:2, :]))
    ri = _iota((L, L), 0)
    ci = _iota((L, L), 1)
    causal = ri >= ci
    acum = jnp.dot(jnp.where(causal, 1.0, 0.0), da, precision=hi, preferred_element_type=F32)
    acum_t = acum.T
    alast = acum[L - 1:L, :]

    xdt = xs * _per_head(dt)
    xw = xdt * _per_head(jnp.exp(alast - acum))
    xw_t = jnp.concatenate([xw[:, j * LANES:(j + 1) * LANES].T for j in range(M_WIDTH // LANES)], axis=0)
    h_in = h_sc[...]
    gw = M_WIDTH // M_GROUPS

    y_off, states, cb = [], [], []
    for g in range(M_GROUPS):
        bg = bm[:, g * M_STATE:(g + 1) * M_STATE]
        cg = cm[:, g * M_STATE:(g + 1) * M_STATE]
        states.append(jnp.dot(xw_t[g * gw:(g + 1) * gw, :].astype(BF16), bg, preferred_element_type=F32))
        y_off.append(_nt_dot(cg, h_in[g * gw:(g + 1) * gw, :].astype(BF16)))
        cb.append(_nt_dot(cg, bg))
    y = jnp.concatenate(y_off, axis=1) * _per_head(jnp.exp(acum))

    lane = _iota((L, LANES), 1)
    y_diag = []
    for j in range(M_HEADS // 2):
        xpair = xdt[:, j * LANES:(j + 1) * LANES]
        acc = jnp.zeros((L, LANES), F32)
        for e in range(2):
            h = 2 * j + e
            seg = acum[:, SM_DT + h:SM_DT + h + 1] - acum_t[SM_DT + h:SM_DT + h + 1, :]
            mh = (cb[h // (M_HEADS // M_GROUPS)] * jnp.exp(jnp.where(causal, seg, -jnp.inf))).astype(BF16)
            keep = (lane < M_HEAD_DIM) if e == 0 else (lane >= M_HEAD_DIM)
            acc = acc + jnp.dot(mh, jnp.where(keep, xpair, 0.0).astype(BF16), preferred_element_type=F32)
        y_diag.append(acc)
    y = y + jnp.concatenate(y_diag, axis=1) + xs * dexp_ref[...]

    cdec = jnp.concatenate(
        [jnp.broadcast_to(acum_t[SM_DT + h:SM_DT + h + 1, L - 1:L], (M_HEAD_DIM, M_STATE))
         for h in range(M_HEADS)], axis=0)
    h_new = h_in * jnp.exp(cdec) + jnp.concatenate(states, axis=0)
    h_sc[...] = h_new

    @pl.when(c == nsteps - 1)
    def _():
        hout_ref[...] = h_new

    y = y * (zz * _sigmoid(zz))
    outs = []
    for g in range(M_GROUPS):
        seg = y[:, g * gw:(g + 1) * gw]
        ms = jnp.mean(seg * seg, axis=-1, keepdims=True)
        outs.append(seg * lax.rsqrt(ms + EPS) * ng_ref[:, g * gw:(g + 1) * gw])
    y_ref[...] = jnp.concatenate(outs, axis=1)[0:rows].astype(y_ref.dtype)


def _mamba(p3, conv_prev, h0, prm, n_valid):
    b, tp, _ = p3.shape
    rows = min(tp, SSD_L)
    nsteps = tp // rows
    hb = rows // SUBLANES
    kern = functools.partial(_mamba_kernel, rows=rows, n_valid=n_valid, nsteps=nsteps)
    full2 = lambda shape: pl.BlockSpec(shape, lambda bb, c: (0, 0))
    y, hout = pl.pallas_call(
        kern,
        grid=(b, nsteps),
        in_specs=[pl.BlockSpec((None, rows, M_CONV_CH), lambda bb, c: (bb, c, C_XBC // M_CONV_CH)),
                  pl.BlockSpec((None, SUBLANES, M_CONV_CH),
                               lambda bb, c: (bb, jnp.maximum(c * hb - 1, 0), C_XBC // M_CONV_CH)),
                  pl.BlockSpec((None, rows, M_WIDTH), lambda bb, c: (bb, c, C_Z // M_WIDTH)),
                  pl.BlockSpec((None, rows, LANES), lambda bb, c: (bb, c, C_SM // LANES)),
                  pl.BlockSpec((None, M_CONV - 1, M_CONV_CH), lambda bb, c: (bb, 0, 0)),
                  pl.BlockSpec((None, M_WIDTH, M_STATE), lambda bb, c: (bb, 0, 0)),
                  full2((M_CONV, M_CONV_CH)), full2((1, M_CONV_CH)), full2((2, LANES)),
                  full2((1, M_WIDTH)), full2((1, M_WIDTH))],
        out_specs=[pl.BlockSpec((None, rows, M_WIDTH), lambda bb, c: (bb, c, 0)),
                   pl.BlockSpec((None, M_WIDTH, M_STATE), lambda bb, c: (bb, 0, 0))],
        out_shape=[jax.ShapeDtypeStruct((b, tp, M_WIDTH), BF16),
                   jax.ShapeDtypeStruct((b, M_WIDTH, M_STATE), F32)],
        scratch_shapes=[pltpu.VMEM((M_WIDTH, M_STATE), F32)],
        compiler_params=_cparams(("parallel", "arbitrary")),
        name="mamba",
    )(p3, p3, p3, p3, conv_prev.astype(F32), h0.reshape(b, M_WIDTH, M_STATE).astype(F32),
      prm["m_conv_w"], prm["m_conv_b"], prm["m_hp"], prm["m_dexp"], prm["m_ng"])
    return y, hout.reshape(b, M_HEADS, M_HEAD_DIM, M_STATE)


def _gelu_tanh(x):
    return 0.5 * x * (1.0 + jnp.tanh(math.sqrt(2.0 / math.pi) * (x + 0.044715 * (x * x * x))))


def _s5_kernel(u_ref, h0r_ref, h0i_ref, lam_ref, bre_ref, bim_ref, cre_ref, cim_ref, dg_ref, gw_ref,
               o_ref, sr_ref, si_ref, cr_sc, ci_sc, *, rows, n_valid):
    t = pl.program_id(1)

    @pl.when(t == 0)
    def _():
        cr_sc[...] = h0r_ref[...]
        ci_sc[...] = h0i_ref[...]

    lr, li = lam_ref[0:1, :], lam_ref[1:2, :]
    step = jnp.exp(lam_ref[2:3, :])
    mag = jnp.exp(lr * step)
    ar, ai = mag * jnp.cos(li * step), mag * jnp.sin(li * step)
    den = lr * lr + li * li
    gr = ((ar - 1.0) * lr + ai * li) / den
    gi = (ai * lr - (ar - 1.0) * li) / den

    nslab = S5_WIDTH // S5_KB
    nb = S5_N // nslab

    def in_proj(x, w_ref):
        return jnp.concatenate(
            [jnp.dot(x[:, k * S5_KB:(k + 1) * S5_KB], w_ref[k * S5_KB:(k + 1) * S5_KB, k * nb:(k + 1) * nb],
                     preferred_element_type=F32) for k in range(nslab)], axis=1)

    def out_proj(x, w_ref):
        return jnp.concatenate(
            [jnp.dot(x[:, k * nb:(k + 1) * nb], w_ref[k * nb:(k + 1) * nb, k * S5_KB:(k + 1) * S5_KB],
                     preferred_element_type=F32) for k in range(nslab)], axis=1)

    u = u_ref[...]
    ub = u.astype(BF16)
    bur = in_proj(ub, bre_ref)
    bui = in_proj(ub, bim_ref)

    seg = rows // SUBLANES
    shape8 = (SUBLANES, S5_N)
    sub = _iota(shape8, 0)
    a8r, a8i = jnp.broadcast_to(ar, shape8), jnp.broadcast_to(ai, shape8)
    g8r, g8i = jnp.broadcast_to(gr, shape8), jnp.broadcast_to(gi, shape8)

    hr_t, hi_t = [], []
    for j in range(seg):
        xr = bur[j * SUBLANES:(j + 1) * SUBLANES, :]
        xi = bui[j * SUBLANES:(j + 1) * SUBLANES, :]
        tr = g8r * xr - g8i * xi
        ti = g8r * xi + g8i * xr
        if j > 0:
            tr, ti = tr + a8r * hr_t[-1] - a8i * hi_t[-1], ti + a8r * hi_t[-1] + a8i * hr_t[-1]
        hr_t.append(tr)
        hi_t.append(ti)

    mr, mi = a8r, a8i
    n = 1
    while n < seg:
        mr, mi = mr * mr - mi * mi, 2.0 * mr * mi
        n *= 2
    c0r = jnp.broadcast_to(cr_sc[...], shape8)
    c0i = jnp.broadcast_to(ci_sc[...], shape8)
    er = hr_t[-1] + jnp.where(sub == 0, mr * c0r - mi * c0i, 0.0)
    ei = hi_t[-1] + jnp.where(sub == 0, mr * c0i + mi * c0r, 0.0)
    d = 1
    while d < SUBLANES:
        sr = jnp.where(sub >= d, pltpu.roll(er, d, 0), 0.0)
        si = jnp.where(sub >= d, pltpu.roll(ei, d, 0), 0.0)
        er, ei = er + mr * sr - mi * si, ei + mr * si + mi * sr
        mr, mi = mr * mr - mi * mi, 2.0 * mr * mi
        d *= 2
    wr = jnp.where(sub == 0, c0r, pltpu.roll(er, 1, 0))
    wi = jnp.where(sub == 0, c0i, pltpu.roll(ei, 1, 0))
    for j in range(seg):
        wr, wi = a8r * wr - a8i * wi, a8r * wi + a8i * wr
        hr_t[j] = hr_t[j] + wr
        hi_t[j] = hi_t[j] + wi

    cr_sc[...] = er[SUBLANES - 1:SUBLANES, :]
    ci_sc[...] = ei[SUBLANES - 1:SUBLANES, :]
    r_last = (n_valid - 1) % rows
    s_last, j_last = r_last // seg, r_last % seg

    @pl.when(t == (n_valid - 1) // rows)
    def _():
        sr_ref[...] = hr_t[j_last][s_last:s_last + 1, :]
        si_ref[...] = hi_t[j_last][s_last:s_last + 1, :]

    hr = jnp.concatenate(hr_t, axis=0)
    hi = jnp.concatenate(hi_t, axis=0)
    y = out_proj(hr.astype(BF16), cre_ref) - out_proj(hi.astype(BF16), cim_ref) + dg_ref[0:1, :] * u
    g = _gelu_tanh(y)
    gate = jnp.dot(g.astype(BF16), gw_ref[...], preferred_element_type=F32) + dg_ref[1:2, :]
    o_ref[...] = (g * _sigmoid(gate)).astype(o_ref.dtype)


def _s5(p3, h0_re, h0_im, prm, n_valid):
    b, tp, _ = p3.shape
    rows = min(tp, 256)
    nsteps = tp // rows
    seg = rows // SUBLANES

    def permute(x, a, c):
        return x.reshape(b, nsteps, a, c, S5_WIDTH).swapaxes(2, 3).reshape(b, tp, S5_WIDTH)

    u3 = p3[:, :, C_U:C_U + S5_WIDTH]
    if seg > 1:
        u3 = permute(u3, SUBLANES, seg)
    kern = functools.partial(_s5_kernel, rows=rows, n_valid=n_valid)
    full2 = lambda shape: pl.BlockSpec(shape, lambda bb, t: (0, 0))
    state_spec = pl.BlockSpec((None, 1, S5_N), lambda bb, t: (bb, 0, 0))
    out, sr, si = pl.pallas_call(
        kern,
        grid=(b, nsteps),
        in_specs=[pl.BlockSpec((None, rows, S5_WIDTH), lambda bb, t: (bb, t, 0)),
                  state_spec, state_spec, full2((3, S5_N)),
                  full2((S5_WIDTH, S5_N)), full2((S5_WIDTH, S5_N)),
                  full2((S5_N, S5_WIDTH)), full2((S5_N, S5_WIDTH)),
                  full2((2, S5_WIDTH)), full2((S5_WIDTH, S5_WIDTH))],
        out_specs=[pl.BlockSpec((None, rows, S5_WIDTH), lambda bb, t: (bb, t, 0)), state_spec, state_spec],
        out_shape=[jax.ShapeDtypeStruct((b, tp, S5_WIDTH), BF16),
                   jax.ShapeDtypeStruct((b, 1, S5_N), F32),
                   jax.ShapeDtypeStruct((b, 1, S5_N), F32)],
        scratch_shapes=[pltpu.VMEM((1, S5_N), F32), pltpu.VMEM((1, S5_N), F32)],
        compiler_params=_cparams(("parallel", "arbitrary")),
        name="s5",
    )(u3, h0_re.reshape(b, 1, S5_N).astype(F32), h0_im.reshape(b, 1, S5_N).astype(F32),
      prm["s5_lam"], prm["s5_bre"], prm["s5_bim"], prm["s5_cre"], prm["s5_cim"], prm["s5_dg"], prm["s5_gw"])
    if seg > 1:
        out = permute(out, seg, SUBLANES)
    return out, sr.reshape(b, S5_GROUPS, S5_STATE), si.reshape(b, S5_GROUPS, S5_STATE)


def _block_diag_in(w):
    g, n, k = w.shape
    eye = jnp.eye(g, dtype=w.dtype)
    return jnp.einsum('gnk,gh->gkhn', w, eye).reshape(g * k, g * n)


def _block_diag_out(w):
    g, k, n = w.shape
    eye = jnp.eye(g, dtype=w.dtype)
    return jnp.einsum('gkn,gh->gnhk', w, eye).reshape(g * n, g * k)


def _outproj_kernel(att_ref, mam_ref, s5_ref, w_ref, res_ref, g_ref, o_ref, hn_ref):
    acc = jnp.dot(att_ref[...], w_ref[0:ATT_WIDTH, :], preferred_element_type=F32)
    acc = acc + jnp.dot(mam_ref[...], w_ref[ATT_WIDTH:ATT_WIDTH + M_WIDTH, :], preferred_element_type=F32)
    acc = acc + jnp.dot(s5_ref[...], w_ref[ATT_WIDTH + M_WIDTH:, :], preferred_element_type=F32)
    x = res_ref[...] + acc
    o_ref[...] = x
    ms = jnp.mean(x * x, axis=-1, keepdims=True)
    hn_ref[...] = (x * lax.rsqrt(ms + EPS) * g_ref[...]).astype(hn_ref.dtype)


def _outproj(att, mam, s5o, w_all, layer, res, g):
    m = att.shape[0]
    tm = min(m, 512)
    row = lambda width: pl.BlockSpec((tm, width), lambda i: (i, 0))
    return pl.pallas_call(
        _outproj_kernel,
        grid=(m // tm,),
        in_specs=[row(ATT_WIDTH), row(M_WIDTH), row(S5_WIDTH),
                  pl.BlockSpec((None, D_MODEL, D_MODEL), lambda i: (layer, 0, 0)),
                  row(D_MODEL), pl.BlockSpec((1, D_MODEL), lambda i: (0, 0))],
        out_specs=[row(D_MODEL), row(D_MODEL)],
        out_shape=[jax.ShapeDtypeStruct((m, D_MODEL), F32), jax.ShapeDtypeStruct((m, D_MODEL), BF16)],
        compiler_params=_cparams(("parallel",)),
        name="outproj",
    )(att, mam, s5o, w_all, res, g.reshape(1, D_MODEL).astype(F32))


def _ffn_conv(x, prev, w_ref, b_ref):
    y = b_ref[...] + x * w_ref[FFN_CONV - 1:FFN_CONV, :]
    for d in range(1, FFN_CONV):
        y = y + _shift_rows(x, prev, d) * w_ref[FFN_CONV - 1 - d:FFN_CONV - d, :]
    return y


def _ffn_fused_kernel(h_ref, wg_ref, wv_ref, pg_ref, pv_ref, cwg_ref, cwv_ref, cbg_ref, cbv_ref,
                      act_ref, fg_ref, fv_ref, cg_sc, cv_sc, *, tiles_per_seq):
    first = (pl.program_id(1) % tiles_per_seq) == 0
    h = h_ref[...]
    tm = h.shape[0]

    def branch(w_ref, prev_ref, carry_sc, cw_ref, cb_ref, f_ref):
        x = jnp.dot(h, w_ref[...], preferred_element_type=F32)
        prev = jnp.where(first, prev_ref[...], carry_sc[...])
        y = _ffn_conv(x, prev, cw_ref, cb_ref)
        tail = x[tm - (FFN_CONV - 1):tm, :]
        carry_sc[...] = tail
        f_ref[...] = tail
        return y

    gate = branch(wg_ref, pg_ref, cg_sc, cwg_ref, cbg_ref, fg_ref)
    val = branch(wv_ref, pv_ref, cv_sc, cwv_ref, cbv_ref, fv_ref)
    act_ref[...] = (gate * _sigmoid(gate) * val).astype(act_ref.dtype)


def _ffn_fused(h2, w_up_all, layer, prev, conv_w, conv_b, b, tp):
    m, d = h2.shape
    tm = min(tp, FFN_TM)
    tps = tp // tm
    tn = 512
    nj = D_FF // tn
    nc = FFN_CONV - 1
    col = lambda o: (lambda j, i: (0, j + o))
    wcol = lambda o: (lambda j, i: (layer, 0, j + o))
    prv = lambda o: pl.BlockSpec((None, nc, tn), lambda j, i: (i // tps, 0, j + o))
    kern = functools.partial(_ffn_fused_kernel, tiles_per_seq=tps)
    act, fg, fv = pl.pallas_call(
        kern,
        grid=(nj, m // tm),
        in_specs=[pl.BlockSpec((tm, d), lambda j, i: (i, 0)),
                  pl.BlockSpec((None, d, tn), wcol(0)), pl.BlockSpec((None, d, tn), wcol(nj)),
                  prv(0), prv(nj),
                  pl.BlockSpec((FFN_CONV, tn), col(0)), pl.BlockSpec((FFN_CONV, tn), col(nj)),
                  pl.BlockSpec((1, tn), col(0)), pl.BlockSpec((1, tn), col(nj))],
        out_specs=[pl.BlockSpec((tm, tn), lambda j, i: (i, j)),
                   pl.BlockSpec((None, nc, tn), lambda j, i: (i // tps, 0, j)),
                   pl.BlockSpec((None, nc, tn), lambda j, i: (i // tps, 0, j))],
        out_shape=[jax.ShapeDtypeStruct((m, D_FF), BF16),
                   jax.ShapeDtypeStruct((b, nc, D_FF), F32),
                   jax.ShapeDtypeStruct((b, nc, D_FF), F32)],
        scratch_shapes=[pltpu.VMEM((nc, tn), F32), pltpu.VMEM((nc, tn), F32)],
        compiler_params=_cparams(("parallel", "arbitrary")),
        name="ffn_up_act",
    )(h2, w_up_all, w_up_all, prev, prev, conv_w, conv_w, conv_b, conv_b)
    return act, jnp.concatenate([fg, fv], axis=-1)


def _matmul_kernel(a_ref, w_ref, o_ref):
    o_ref[...] = jnp.dot(a_ref[...], w_ref[...], preferred_element_type=F32).astype(o_ref.dtype)


def _ffn_up(a, w_all, layer, tn=512):
    m, k = a.shape
    n = w_all.shape[2]
    tm = min(m, 1024)
    return pl.pallas_call(
        _matmul_kernel,
        grid=(m // tm, n // tn),
        in_specs=[pl.BlockSpec((tm, k), lambda i, j: (i, 0)),
                  pl.BlockSpec((None, k, tn), lambda i, j: (layer, 0, j))],
        out_specs=pl.BlockSpec((tm, tn), lambda i, j: (i, j)),
        out_shape=jax.ShapeDtypeStruct((m, n), F32),
        compiler_params=_cparams(("parallel", "arbitrary")),
        name="ffn_up",
    )(a, w_all)


def _ffn_act_kernel(g_ref, v_ref, gp_ref, vp_ref, gw_ref, vw_ref, gb_ref, vb_ref, o_ref):
    gate = _ffn_conv(g_ref[...], gp_ref[...], gw_ref, gb_ref)
    val = _ffn_conv(v_ref[...], vp_ref[...], vw_ref, vb_ref)
    o_ref[...] = (gate * _sigmoid(gate) * val).astype(o_ref.dtype)


def _ffn_act_short(up3, prev, conv_w, conv_b):
    b, tp, _ = up3.shape
    tc = 512
    nj = D_FF // tc
    nc = FFN_CONV - 1
    main = lambda o: pl.BlockSpec((None, tp, tc), lambda bb, j: (bb, 0, j + o))
    prv = lambda o: pl.BlockSpec((None, nc, tc), lambda bb, j: (bb, 0, j + o))
    wsp = lambda o: pl.BlockSpec((FFN_CONV, tc), lambda bb, j: (0, j + o))
    bsp = lambda o: pl.BlockSpec((1, tc), lambda bb, j: (0, j + o))
    return pl.pallas_call(
        _ffn_act_kernel,
        grid=(b, nj),
        in_specs=[main(0), main(nj), prv(0), prv(nj), wsp(0), wsp(nj), bsp(0), bsp(nj)],
        out_specs=pl.BlockSpec((None, tp, tc), lambda bb, j: (bb, 0, j)),
        out_shape=jax.ShapeDtypeStruct((b, tp, D_FF), BF16),
        compiler_params=_cparams(("parallel", "parallel")),
        name="ffn_act",
    )(up3, up3, prev, prev, conv_w, conv_w, conv_b, conv_b)


def _down_kernel(a_ref, w_ref, res_ref, o_ref):
    o_ref[...] = res_ref[...] + jnp.dot(a_ref[...], w_ref[...], preferred_element_type=F32)


def _ffn_down(act, w_all, layer, res):
    m, k = act.shape
    tm = min(m, 1024)
    tn = 256
    return pl.pallas_call(
        _down_kernel,
        grid=(m // tm, D_MODEL // tn),
        in_specs=[pl.BlockSpec((tm, k), lambda i, j: (i, 0)),
                  pl.BlockSpec((None, k, tn), lambda i, j: (layer, 0, j)),
                  pl.BlockSpec((tm, tn), lambda i, j: (i, j))],
        out_specs=pl.BlockSpec((tm, tn), lambda i, j: (i, j)),
        out_shape=jax.ShapeDtypeStruct((m, D_MODEL), F32),
        compiler_params=_cparams(("parallel", "arbitrary")),
        name="ffn_down",
    )(act, w_all, res)


def _prep_layer_params(l, w_in, m_conv_w, m_conv_b, m_dt_bias, m_a_log, m_d, m_norm_g,
                       s5_lam_re, s5_lam_im, s5_log_step, s5_b_re, s5_b_im, s5_c_re, s5_c_im,
                       s5_d, s5_glu_w, s5_glu_b, w_out, ffn_w_up, ffn_conv_w, ffn_conv_b, ffn_w_down):
    def lanes(v):
        return jnp.zeros((LANES,), F32).at[SM_DT:SM_DT + M_HEADS].set(v.astype(F32))

    return {
        "w_in": _prep_w_in(w_in[l]),
        "m_conv_w": m_conv_w[l].astype(F32),
        "m_conv_b": m_conv_b[l].reshape(1, M_CONV_CH).astype(F32),
        "m_hp": jnp.stack([lanes(m_dt_bias[l]), lanes(m_a_log[l])], axis=0),
        "m_dexp": jnp.repeat(m_d[l].astype(F32), M_HEAD_DIM).reshape(1, M_WIDTH),
        "m_ng": m_norm_g[l].reshape(1, M_WIDTH).astype(F32),
        "s5_lam": jnp.stack([s5_lam_re[l].reshape(S5_N), s5_lam_im[l].reshape(S5_N),
                             jnp.repeat(s5_log_step[l], S5_STATE)], axis=0).astype(F32),
        "s5_bre": _block_diag_in(s5_b_re[l]).astype(BF16),
        "s5_bim": _block_diag_in(s5_b_im[l]).astype(BF16),
        "s5_cre": _block_diag_out(s5_c_re[l]).astype(BF16),
        "s5_cim": _block_diag_out(s5_c_im[l]).astype(BF16),
        "s5_dg": jnp.stack([s5_d[l], s5_glu_b[l]], axis=0).astype(F32),
        "s5_gw": s5_glu_w[l].astype(BF16),
        "ffn_conv_w": ffn_conv_w[l].astype(F32),
        "ffn_conv_b": ffn_conv_b[l].reshape(1, 2 * D_FF).astype(F32),
    }


def _layer(x3, rt, attend, mconv_prev, ssm_h0, s5_h0_re, s5_h0_im, fconv_prev, norm1_g, norm2_g, prm, wts,
           layer, depth, kv_prev, n_valid):
    b, tp, d = x3.shape
    m = b * tp
    x2 = x3.reshape(m, d)
    w_out_all, w_up_all, w_down_all = wts
    h = _rmsnorm(x2, norm1_g, BF16)
    p2, qkv2, k_all, v_all = _inproj(h, prm["w_in"], rt, layer, depth, kv_prev)
    p3 = p2.reshape(b, tp, N_P)
    att = attend(p3, qkv2.reshape(b, tp, N_QKV))
    mam, ssm_new = _mamba(p3, mconv_prev, ssm_h0, prm, n_valid)
    s5o, s5_re, s5_im = _s5(p3, s5_h0_re, s5_h0_im, prm, n_valid)
    x2, h2 = _outproj(att.reshape(m, ATT_WIDTH), mam.reshape(m, M_WIDTH), s5o.reshape(m, S5_WIDTH),
                      w_out_all, layer, x2, norm2_g)
    fconv_prev = fconv_prev.astype(F32)
    if tp >= SSD_L:
        act, fconv_new = _ffn_fused(h2, w_up_all, layer, fconv_prev, prm["ffn_conv_w"], prm["ffn_conv_b"], b, tp)
    else:
        up3 = _ffn_up(h2, w_up_all, layer).reshape(b, tp, 2 * D_FF)
        act = _ffn_act_short(up3, fconv_prev, prm["ffn_conv_w"], prm["ffn_conv_b"]).reshape(m, D_FF)
        fconv_new = jnp.concatenate([fconv_prev, up3[:, :n_valid]], axis=1)[:, n_valid:]
    x2 = _ffn_down(act, w_down_all, layer, x2)

    ik = p3[:, :n_valid, C_IK:C_IK + IDX_DIM]
    xbc_raw = jnp.concatenate([mconv_prev.astype(F32), p3[:, :n_valid, C_XBC:C_XBC + M_CONV_CH]], axis=1)
    mconv_new = xbc_raw[:, n_valid:]
    return x2.reshape(b, tp, d), (k_all, v_all), (ik, ssm_new, mconv_new, s5_re, s5_im, fconv_new)


def kernel(x_prompt, x_sample, cache_k, cache_v, cache_idx_k, state_ssm, state_mconv, state_s5_re, state_s5_im,
           state_fconv, page_table, norm1_g, w_in, m_conv_w, m_conv_b, m_dt_bias, m_a_log, m_d, m_norm_g,
           s5_lam_re, s5_lam_im, s5_log_step, s5_b_re, s5_b_im, s5_c_re, s5_c_im, s5_d, s5_glu_w, s5_glu_b,
           w_out, norm2_g, ffn_w_up, ffn_conv_w, ffn_conv_b, ffn_w_down, final_norm_g):
    bp, seq, d = x_prompt.shape
    bs, t_new, _ = x_sample.shape
    depth = w_in.shape[0]
    npages = page_table.shape[1]
    past = npages * PAGE_SIZE
    tpad = SUBLANES
    assert t_new <= tpad and seq % SSD_L == 0

    rt_p = _rope_tables(jnp.arange(seq, dtype=I32))
    if seq % PROJ_TM != 0:
        rt_p = jnp.tile(rt_p, (bp, 1))
    rt_s = jnp.tile(_rope_tables(past + jnp.arange(tpad, dtype=I32)), (bs, 1))
    hp = x_prompt.astype(F32)
    hs = jnp.pad(x_sample.astype(F32), ((0, 0), (0, tpad - t_new), (0, 0)))
    n_pool = cache_k.shape[1]
    pool_k = cache_k.reshape(depth, n_pool, PAGE_SIZE * ATT_HEADS, HEAD_DIM)
    pool_v = cache_v.reshape(depth, n_pool, PAGE_SIZE * ATT_HEADS, HEAD_DIM)
    topk_p = min(TOPK_MAX, seq // 4)
    topk_s = min(TOPK_MAX, (past + t_new) // 4)

    zero_mconv = jnp.zeros((bp, M_CONV - 1, M_CONV_CH), F32)
    zero_ssm = jnp.zeros((bp, M_HEADS, M_HEAD_DIM, M_STATE), F32)
    zero_s5 = jnp.zeros((bp, S5_GROUPS, S5_STATE), F32)
    zero_fconv = jnp.zeros((bp, FFN_CONV - 1, 2 * D_FF), F32)

    pool_ik_t = cache_idx_k.transpose(0, 1, 3, 2)
    wts = (w_out.astype(BF16), ffn_w_up.astype(BF16), ffn_w_down.astype(BF16))

    def attend_prompt(p3, qkv3):
        return _prompt_attention(qkv3, _prompt_index_bias(p3, topk_p))

    new_p, new_s = [], []
    kv_p = kv_s = None
    for l in range(depth):
        prm = _prep_layer_params(l, w_in, m_conv_w, m_conv_b, m_dt_bias, m_a_log, m_d, m_norm_g,
                                 s5_lam_re, s5_lam_im, s5_log_step, s5_b_re, s5_b_im, s5_c_re, s5_c_im,
                                 s5_d, s5_glu_w, s5_glu_b, w_out, ffn_w_up, ffn_conv_w, ffn_conv_b, ffn_w_down)

        def attend_sample(p3, qkv3, l=l):
            iq = p3[:, :, C_IQ:C_IQ + IDX_HEADS * IDX_DIM].reshape(bs, tpad, IDX_HEADS, IDX_DIM)
            iq_rows = iq.transpose(0, 2, 1, 3).reshape(bs, IDX_HEADS * tpad, IDX_DIM).astype(BF16)
            iw = p3[:, :, C_SM + SM_IW:C_SM + SM_IW + IDX_HEADS] * ((IDX_HEADS ** -0.5) * (IDX_DIM ** -0.5))
            w_col = iw.transpose(0, 2, 1).reshape(bs, IDX_HEADS * tpad, 1)
            pad_rows = ((0, 0), (0, LANES - tpad), (0, 0))
            ik_new_t = jnp.pad(p3[:, :, C_IK:C_IK + IDX_DIM], pad_rows).transpose(0, 2, 1)
            mask = _sample_index_mask(page_table, iq_rows, w_col, ik_new_t, pool_ik_t, l, t_new, topk_s)
            q = qkv3[:, :, C_Q:C_Q + ATT_WIDTH].reshape(bs, tpad, ATT_HEADS, HEAD_DIM).transpose(0, 2, 1, 3)
            eye = jnp.eye(ATT_HEADS, dtype=BF16)
            qbd = (q[:, :, :, None, :] * eye[None, :, None, :, None]).reshape(bs, ATT_HEADS * tpad, ATT_WIDTH)
            k_new = jnp.pad(qkv3[:, :, C_K:C_K + ATT_WIDTH], pad_rows)
            v_new = jnp.pad(qkv3[:, :, C_V:C_V + ATT_WIDTH], pad_rows)
            return _sample_attention(page_table, qbd, mask, k_new, v_new, pool_k, pool_v, l)

        hp, kv_p, st_p = _layer(hp, rt_p, attend_prompt, zero_mconv, zero_ssm, zero_s5, zero_s5, zero_fconv,
                                norm1_g[l], norm2_g[l], prm, wts, l, depth, kv_p, seq)
        hs, kv_s, st_s = _layer(hs, rt_s, attend_sample, state_mconv[l], state_ssm[l], state_s5_re[l],
                                state_s5_im[l], state_fconv[l], norm1_g[l], norm2_g[l], prm, wts, l, depth,
                                kv_s, t_new)
        new_p.append(st_p)
        new_s.append(st_s)

    def stacked(states, i):
        return jnp.stack([s[i] for s in states], axis=0)

    def heads(kv, b, tp, n_valid):
        return kv.reshape(depth, b, tp, ATT_HEADS, HEAD_DIM)[:, :, :n_valid]

    y_prompt = _rmsnorm(hp.reshape(bp * seq, d), final_norm_g, F32).reshape(bp, seq, d)
    y_sample = _rmsnorm(hs.reshape(bs * tpad, d), final_norm_g, F32).reshape(bs, tpad, d)[:, :t_new]
    return (y_prompt, y_sample,
            heads(kv_p[0], bp, seq, seq), heads(kv_p[1], bp, seq, seq),
            stacked(new_p, 0), stacked(new_p, 1), stacked(new_p, 2), stacked(new_p, 3),
            stacked(new_p, 4), stacked(new_p, 5),
            heads(kv_s[0], bs, tpad, t_new), heads(kv_s[1], bs, tpad, t_new),
            stacked(new_s, 0), stacked(new_s, 1), stacked(new_s, 2), stacked(new_s, 3),
            stacked(new_s, 4), stacked(new_s, 5))
```
